```python
import jax, jax.numpy as jnp
from jax import lax
import numpy as np

D_MODEL = 1024
BATCH = 8
SEQ = 2048
DEPTH = 1

EPS = 1e-6
NEG_INF = -1e30

N_ATTN_HEADS = 8
ATTN_HEAD_DIM = 64
ATTN_V_DIM = 2 * ATTN_HEAD_DIM
QK_WIDTH = N_ATTN_HEADS * 2 * ATTN_HEAD_DIM
ATTN_WIDTH = N_ATTN_HEADS * ATTN_V_DIM
ROT_DIM = ATTN_HEAD_DIM // 4
ROPE_THETA = 500000.0
Q_BLOCK = 128

POOL_WINDOWS = (2, 4, 8, 16)
N_POOL_GROUPS = len(POOL_WINDOWS)
POOL_WIDTH = D_MODEL
POOL_GROUP_DIM = POOL_WIDTH // N_POOL_GROUPS

IN_WIDTH = 2 * QK_WIDTH + ATTN_WIDTH + POOL_WIDTH
N_BRANCHES = 2
N_MOD = 6

PEER_HEADS = 8
N_KEYS = 128
N_EXPERTS = N_KEYS * N_KEYS
PEER_QUERY_DIM = 256
PEER_HALF = PEER_QUERY_DIM // 2
PEER_TOPK = 16
PEER_TOKEN_BLOCK = 128

kernel_name = "hybrid_pool_diffattn_peer_block"


def rms_norm(x, gain):
    xf = x.astype(jnp.float32)
    y = xf * lax.rsqrt(jnp.mean(xf * xf, axis=-1, keepdims=True) + EPS)
    return (y * gain.astype(jnp.float32)).astype(x.dtype)


def rope_partial(t, cos, sin):
    half = ROT_DIM // 2
    c = cos[:, :, None, None, :]
    s = sin[:, :, None, None, :]
    t1 = t[..., :half].astype(jnp.float32)
    t2 = t[..., half:ROT_DIM].astype(jnp.float32)
    rot = jnp.concatenate([t1 * c - t2 * s, t2 * c + t1 * s], axis=-1).astype(t.dtype)
    return jnp.concatenate([rot, t[..., ROT_DIM:]], axis=-1)


def causal_multiscale_pool(xp):
    S = xp.shape[1]
    xf = xp.astype(jnp.float32)
    cs = jnp.cumsum(xf, axis=1)
    t = jnp.arange(S)
    outs = []
    for g, w in enumerate(POOL_WINDOWS):
        csg = cs[:, :, g]
        lag = jnp.pad(csg, ((0, 0), (w, 0), (0, 0)))[:, :S]
        cnt = jnp.minimum(t + 1, w).astype(jnp.float32)[None, :, None]
        outs.append((csg - lag) / cnt - xf[:, :, g])
    return jnp.stack(outs, axis=2).astype(xp.dtype)


def diff_attention(q, k, v, lam):
    S = q.shape[1]
    scale = ATTN_HEAD_DIM ** -0.5
    outs = []
    for i in range(S // Q_BLOCK):
        lo, hi = i * Q_BLOCK, (i + 1) * Q_BLOCK
        qb, kb, vb = q[:, lo:hi], k[:, :hi], v[:, :hi]
        s = jnp.einsum('bqhcd,bkhcd->bhcqk', qb, kb).astype(jnp.float32) * scale
        mask = (lo + jnp.arange(Q_BLOCK))[:, None] >= jnp.arange(hi)[None, :]
        p = jax.nn.softmax(jnp.where(mask, s, NEG_INF), axis=-1)
        a = p[:, :, 0] - lam * p[:, :, 1]
        outs.append(jnp.einsum('bhqk,bkhd->bqhd', a.astype(vb.dtype), vb))
    return jnp.concatenate(outs, axis=1)


def _peer_block(hb, ib, gb, down, up):
    u = down[ib]
    pre = jnp.einsum('td,thkd->thk', hb, u).astype(jnp.float32)
    w = (gb * jax.nn.gelu(pre, approximate=False)).astype(hb.dtype)
    return jnp.einsum('thk,thkd->td', w, up[ib])


def peer(h, w_query, sub_keys, down, up):
    B, S, D = h.shape
    T = B * S
    hf = h.reshape(T, D)
    q = (hf @ w_query).reshape(T, PEER_HEADS, 2, PEER_HALF)
    s = jnp.einsum('thpd,pkd->thpk', q, sub_keys).astype(jnp.float32)
    sv, si = lax.top_k(s, PEER_TOPK)
    cand = (sv[:, :, 0, :, None] + sv[:, :, 1, None, :]).reshape(T, PEER_HEADS, PEER_TOPK * PEER_TOPK)
    cand_idx = (si[:, :, 0, :, None] * N_KEYS + si[:, :, 1, None, :]).reshape(T, PEER_HEADS, PEER_TOPK * PEER_TOPK)
    top_s, top_pos = lax.top_k(cand, PEER_TOPK)
    idx = jnp.take_along_axis(cand_idx, top_pos, axis=-1)
    g = jax.nn.softmax(top_s, axis=-1)
    nb = T // PEER_TOKEN_BLOCK
    out = lax.map(
        lambda a: _peer_block(a[0], a[1], a[2], down, up),
        (hf.reshape(nb, PEER_TOKEN_BLOCK, D),
         idx.reshape(nb, PEER_TOKEN_BLOCK, PEER_HEADS, PEER_TOPK),
         g.reshape(nb, PEER_TOKEN_BLOCK, PEER_HEADS, PEER_TOPK)))
    return out.reshape(B, S, D)


def setup_inputs(seed: int = 0) -> dict:
    key = jax.random.key(seed)
    ks = jax.random.split(key, 26)
    f32 = jnp.float32
    D, L = D_MODEL, DEPTH

    def nrm(k, shape, scale):
        return jax.random.normal(k, shape, f32) * scale

    return {
        'x': nrm(ks[0], (BATCH, SEQ, D), 1.0),
        'c': nrm(ks[1], (BATCH, D), 1.0),
        'positions': jax.random.randint(ks[2], (BATCH, 1), 0, 4096, jnp.int32) + jnp.arange(SEQ, dtype=jnp.int32)[None, :],
        'norm_mix_gain': 1.0 + nrm(ks[3], (L, D), 0.05),
        'norm_ffn_gain': 1.0 + nrm(ks[4], (L, D), 0.05),
        'w_ada': nrm(ks[5], (L, D, N_MOD * D), D ** -0.5),
        'b_ada': nrm(ks[6], (L, N_MOD * D), 0.02),
        'w_in': nrm(ks[7], (L, D, IN_WIDTH), D ** -0.5),
        'w_gate': nrm(ks[8], (L, D, N_BRANCHES * D), D ** -0.5),
        'b_gate': nrm(ks[9], (L, N_BRANCHES * D), 0.02),
        'pool_w': nrm(ks[10], (L, N_POOL_GROUPS, POOL_GROUP_DIM, POOL_GROUP_DIM), POOL_GROUP_DIM ** -0.5),
        'pool_scale': 1.0 + nrm(ks[11], (L, POOL_WIDTH), 0.1),
        'lambda_q1': nrm(ks[12], (L, ATTN_HEAD_DIM), 0.1),
        'lambda_k1': nrm(ks[13], (L, ATTN_HEAD_DIM), 0.1),
        'lambda_q2': nrm(ks[14], (L, ATTN_HEAD_DIM), 0.1),
        'lambda_k2': nrm(ks[15], (L, ATTN_HEAD_DIM), 0.1),
        'attn_subln_gain': 1.0 + nrm(ks[16], (L, ATTN_WIDTH), 0.05),
        'w_branch_pool': nrm(ks[17], (L, POOL_WIDTH, D), POOL_WIDTH ** -0.5),
        'w_branch_attn': nrm(ks[18], (L, ATTN_WIDTH, D), ATTN_WIDTH ** -0.5),
        'w_out': nrm(ks[19], (L, D, D), D ** -0.5),
        'peer_w_query': nrm(ks[20], (L, D, PEER_HEADS * PEER_QUERY_DIM), D ** -0.5),
        'peer_sub_keys': nrm(ks[21], (L, 2, N_KEYS, PEER_HALF), PEER_HALF ** -0.5),
        'peer_down': nrm(ks[22], (L, N_EXPERTS, D), D ** -0.5),
        'peer_up': nrm(ks[23], (L, N_EXPERTS, D), 0.3),
        'final_norm_gain': 1.0 + nrm(ks[24], (D,), 0.05),
    }


def reference(x, c, positions, norm_mix_gain, norm_ffn_gain, w_ada, b_ada, w_in, w_gate, b_gate,
              pool_w, pool_scale, lambda_q1, lambda_k1, lambda_q2, lambda_k2, attn_subln_gain,
              w_branch_pool, w_branch_attn, w_out, peer_w_query, peer_sub_keys, peer_down, peer_up,
              final_norm_gain):
    B, S, D = x.shape
    inv_freq = ROPE_THETA ** (-jnp.arange(0, ROT_DIM, 2, dtype=jnp.float32) / ROT_DIM)
    ang = positions.astype(jnp.float32)[..., None] * inv_freq
    cos, sin = jnp.cos(ang), jnp.sin(ang)
    c_act = jax.nn.silu(c)

    for l in range(DEPTH):
        mod = (c_act @ w_ada[l] + b_ada[l]).reshape(B, N_MOD, D)
        shift_m, scale_m, gate_m, shift_f, scale_f, gate_f = [mod[:, i, None, :] for i in range(N_MOD)]

        h = rms_norm(x, norm_mix_gain[l]) * (1.0 + scale_m) + shift_m
        proj = h @ w_in[l]
        q, k, v, xp = jnp.split(proj, [QK_WIDTH, 2 * QK_WIDTH, 2 * QK_WIDTH + ATTN_WIDTH], axis=-1)

        pooled = causal_multiscale_pool(xp.reshape(B, S, N_POOL_GROUPS, POOL_GROUP_DIM))
        ya = jnp.einsum('bsgc,gcd->bsgd', pooled, pool_w[l]).reshape(B, S, POOL_WIDTH) * pool_scale[l]
        y_pool = ya @ w_branch_pool[l]

        lam_init = 0.8 - 0.6 * float(np.exp(-0.3 * l))
        lam = (jnp.exp(jnp.sum(lambda_q1[l] * lambda_k1[l]).astype(jnp.float32))
               - jnp.exp(jnp.sum(lambda_q2[l] * lambda_k2[l]).astype(jnp.float32)) + lam_init)
        qh = rope_partial(q.reshape(B, S, N_ATTN_HEADS, 2, ATTN_HEAD_DIM), cos, sin)
        kh = rope_partial(k.reshape(B, S, N_ATTN_HEADS, 2, ATTN_HEAD_DIM), cos, sin)
        vh = v.reshape(B, S, N_ATTN_HEADS, ATTN_V_DIM)
        o = diff_attention(qh, kh, vh, lam)
        o = rms_norm(o, attn_subln_gain[l].reshape(N_ATTN_HEADS, ATTN_V_DIM)) * (1.0 - lam_init)
        y_attn = o.reshape(B, S, ATTN_WIDTH) @ w_branch_attn[l]

        gates = jax.nn.sigmoid(h @ w_gate[l] + b_gate[l]).reshape(B, S, N_BRANCHES, D)
        merged = gates[:, :, 0] * y_pool + gates[:, :, 1] * y_attn
        x = x + gate_m * (merged @ w_out[l])

        h2 = rms_norm(x, norm_ffn_gain[l]) * (1.0 + scale_f) + shift_f
        x = x + gate_f * peer(h2, peer_w_query[l], peer_sub_keys[l], peer_down[l], peer_up[l])

    return rms_norm(x, final_norm_gain)
```

```python
import functools
import math

import jax
import jax.numpy as jnp
from jax import lax
from jax.experimental import pallas as pl
from jax.experimental.pallas import tpu as pltpu

F32 = jnp.float32
BF16 = jnp.bfloat16
I32 = jnp.int32

EPS = 1e-6
NEG_INF = -1e30

N_ATTN_HEADS = 8
ATTN_HEAD_DIM = 64
ATTN_V_DIM = 128
ROT_DIM = 16
ROPE_THETA = 500000.0
POOL_WINDOWS = (2, 4, 8, 16)
POOL_GROUP_DIM = 256
POOL_HALO = 16
N_MOD = 6

PEER_HEADS = 8
N_KEYS = 128
PEER_HALF = 128
PEER_TOPK = 16
N_PAIRS = PEER_HEADS * PEER_TOPK

LANES = 128
MXU_DIM = 256
VMEM_LIMIT = 56 * 1024 * 1024

W_PITCH = 136


def _rms(x, eps=EPS):
    return x * lax.rsqrt(jnp.mean(x * x, axis=-1, keepdims=True) + eps)


def _gelu(x):
    return 0.5 * x * (1.0 + lax.erf(x * math.sqrt(0.5)))


def _ada_kernel(c_ref, w_ref, b_ref, o_ref):
    ca = jax.nn.silu(c_ref[...])
    o_ref[...] = jnp.dot(ca, w_ref[...], preferred_element_type=F32,
                         precision=lax.Precision.HIGHEST) + b_ref[...]


def _ada_mod(c, w, b):
    bsz, d = c.shape
    n = w.shape[1]
    tn = 1536
    return pl.pallas_call(
        _ada_kernel,
        grid=(n // tn,),
        in_specs=[pl.BlockSpec((bsz, d), lambda j: (0, 0)),
                  pl.BlockSpec((d, tn), lambda j: (0, j)),
                  pl.BlockSpec((1, tn), lambda j: (0, j))],
        out_specs=pl.BlockSpec((bsz, tn), lambda j: (0, j)),
        out_shape=jax.ShapeDtypeStruct((bsz, n), F32),
        compiler_params=pltpu.CompilerParams(vmem_limit_bytes=VMEM_LIMIT),
        name="ada_mod",
    )(c, w, b.reshape(1, n))


def _mixer_in_kernel(x_ref, pos_ref, mod_ref, gain_ref, invf_ref, win_ref, wg_ref, bg_ref,
                     q_ref, k_ref, v_ref, xp_ref, gt_ref):
    d = x_ref.shape[-1]
    x = x_ref[0]
    shift = mod_ref[0, 0:1, :]
    scale = mod_ref[0, 1:2, :]
    h = (_rms(x) * gain_ref[...]) * (1.0 + scale) + shift
    hb = h.astype(BF16)

    ang = pos_ref[0] * invf_ref[...]
    cos = jnp.cos(ang)
    sin = jnp.sin(ang)
    lane = lax.broadcasted_iota(I32, (1, LANES), 1) % ATTN_HEAD_DIM
    half = ROT_DIM // 2
    cm = jnp.where(lane < ROT_DIM, cos, 1.0)
    sa = jnp.where(lane < half, -sin, 0.0)
    sb = jnp.where((lane >= half) & (lane < ROT_DIM), sin, 0.0)

    def rope(t):
        return (t * cm + pltpu.roll(t, LANES - half, 1) * sa + pltpu.roll(t, half, 1) * sb)

    qk_scale = ATTN_HEAD_DIM ** -0.5
    for c, (ref, scl) in enumerate(((q_ref, qk_scale), (k_ref, 1.0))):
        p = jnp.dot(hb, win_ref[:, c * d:(c + 1) * d], preferred_element_type=F32)
        for hh in range(d // LANES):
            sl = slice(hh * LANES, (hh + 1) * LANES)
            ref[0, :, sl] = (rope(p[:, sl]) * scl).astype(ref.dtype)
    v_ref[0] = jnp.dot(hb, win_ref[:, 2 * d:3 * d], preferred_element_type=F32).astype(v_ref.dtype)
    xp_ref[0] = jnp.dot(hb, win_ref[:, 3 * d:4 * d], preferred_element_type=F32)
    for c in range(2):
        g = jnp.dot(hb, wg_ref[:, c * d:(c + 1) * d], preferred_element_type=F32)
        gt_ref[0, :, c * d:(c + 1) * d] = jax.nn.sigmoid(
            g + bg_ref[:, c * d:(c + 1) * d]).astype(gt_ref.dtype)


def _mixer_in(x, posf, mod, gain, invf, w_in, w_gate, b_gate, tm):
    bsz, s, d = x.shape
    const = lambda b, i: (0, 0)
    tile = lambda b, i: (b, i, 0)
    one = pl.Buffered(1)
    return pl.pallas_call(
        _mixer_in_kernel,
        grid=(bsz, s // tm),
        in_specs=[pl.BlockSpec((1, tm, d), tile),
                  pl.BlockSpec((1, tm, 1), tile),
                  pl.BlockSpec((1, N_MOD, d), lambda b, i: (b, 0, 0)),
                  pl.BlockSpec((1, d), const),
                  pl.BlockSpec((1, LANES), const),
                  pl.BlockSpec(w_in.shape, const, pipeline_mode=one),
                  pl.BlockSpec(w_gate.shape, const, pipeline_mode=one),
                  pl.BlockSpec((1, 2 * d), const)],
        out_specs=[pl.BlockSpec((1, tm, d), tile)] * 4 + [pl.BlockSpec((1, tm, 2 * d), tile)],
        out_shape=[jax.ShapeDtypeStruct((bsz, s, d), BF16),
                   jax.ShapeDtypeStruct((bsz, s, d), BF16),
                   jax.ShapeDtypeStruct((bsz, s, d), BF16),
                   jax.ShapeDtypeStruct((bsz, s, d), F32),
                   jax.ShapeDtypeStruct((bsz, s, 2 * d), BF16)],
        compiler_params=pltpu.CompilerParams(
            dimension_semantics=("parallel", "parallel"), vmem_limit_bytes=VMEM_LIMIT),
        name="mixer_in",
    )(x, posf, mod, gain, invf, w_in, w_gate, b_gate)


def _attn_kernel(lam_ref, q_ref, k_ref, v_ref, gain_ref, o_ref, *, tq, lam_init):
    qi = pl.program_id(2)
    q = q_ref[0]
    lane = lax.broadcasted_iota(I32, q.shape, 1)
    zero = jnp.zeros_like(q)
    qq = jnp.concatenate([jnp.where(lane < ATTN_HEAD_DIM, q, zero),
                          jnp.where(lane >= ATTN_HEAD_DIM, q, zero)], axis=0)

    def step(j, carry, diagonal):
        m, l, acc = carry
        kb = k_ref[0, pl.ds(pl.multiple_of(j * tq, tq), tq), :]
        vb = v_ref[0, pl.ds(pl.multiple_of(j * tq, tq), tq), :]
        s = lax.dot_general(qq, kb, (((1,), (1,)), ((), ())), preferred_element_type=F32)
        if diagonal:
            row = lax.broadcasted_iota(I32, s.shape, 0) % tq
            col = lax.broadcasted_iota(I32, s.shape, 1)
            s = jnp.where(row >= col, s, NEG_INF)
        m_new = jnp.maximum(m, jnp.max(s, axis=-1, keepdims=True))
        alpha = jnp.exp(m - m_new)
        p = jnp.exp(s - m_new)
        l = alpha * l + jnp.sum(p, axis=-1, keepdims=True)
        acc = alpha * acc + jnp.dot(p.astype(vb.dtype), vb, preferred_element_type=F32)
        return m_new, l, acc

    init = (jnp.full((2 * tq, 1), NEG_INF, F32), jnp.zeros((2 * tq, 1), F32),
            jnp.zeros((2 * tq, ATTN_V_DIM), F32))
    carry = lax.fori_loop(0, qi, functools.partial(step, diagonal=False), init)
    _, l, acc = step(qi, carry, True)

    lv = lam_ref[...]
    lam = (jnp.exp(jnp.sum(lv[0:1] * lv[1:2], axis=-1, keepdims=True))
           - jnp.exp(jnp.sum(lv[2:3] * lv[3:4], axis=-1, keepdims=True)) + lam_init)
    o = acc[:tq] / l[:tq] - lam * (acc[tq:] / l[tq:])
    o_ref[0] = (_rms(o) * gain_ref[...] * (1.0 - lam_init)).astype(o_ref.dtype)


def _diff_attn(lam_vecs, q, k, v, gain, lam_init, tq):
    bsz, s, d = q.shape
    nh = d // LANES
    kv_spec = pl.BlockSpec((1, s, LANES), lambda b, h, i: (b, 0, h))
    return pl.pallas_call(
        functools.partial(_attn_kernel, tq=tq, lam_init=lam_init),
        grid=(bsz, nh, s // tq),
        in_specs=[pl.BlockSpec(lam_vecs.shape, lambda b, h, i: (0, 0)),
                  pl.BlockSpec((1, tq, LANES), lambda b, h, i: (b, i, h)),
                  kv_spec, kv_spec,
                  pl.BlockSpec((1, LANES), lambda b, h, i: (0, h))],
        out_specs=pl.BlockSpec((1, tq, LANES), lambda b, h, i: (b, i, h)),
        out_shape=jax.ShapeDtypeStruct((bsz, s, d), BF16),
        compiler_params=pltpu.CompilerParams(
            dimension_semantics=("parallel", "parallel", "parallel"),
            vmem_limit_bytes=VMEM_LIMIT),
        name="diff_attn",
    )(lam_vecs, q, k, v, gain)


def _merge_kernel(x_ref, xp_ref, halo_ref, o_ref, gt_ref, mod_ref, gain_ref, pw_ref, ps_ref,
                  wbp_ref, wba_ref, wout_ref, x1_ref, h2_ref, xs_ref, *, tm):
    i = pl.program_id(1)
    d = x_ref.shape[-1]
    halo = halo_ref[0]
    xs_ref[0:POOL_HALO, :] = jnp.where(i > 0, halo, jnp.zeros_like(halo))
    xs_ref[POOL_HALO:, :] = xp_ref[0]
    pos = i * tm + lax.broadcasted_iota(I32, (tm, 1), 0)
    y_pool = None
    ya = []
    for g, w in enumerate(POOL_WINDOWS):
        cols = slice(g * POOL_GROUP_DIM, (g + 1) * POOL_GROUP_DIM)
        tok = xs_ref[POOL_HALO:, cols]
        acc = tok
        for u in range(1, w):
            acc = acc + xs_ref[POOL_HALO - u:POOL_HALO - u + tm, cols]
        cnt = jnp.minimum(pos + 1, w).astype(F32)
        pooled = acc / cnt - tok
        yg = jnp.dot(pooled.astype(BF16), pw_ref[g], preferred_element_type=F32)
        ya.append((yg * ps_ref[:, cols]).astype(BF16))
    ya = jnp.concatenate(ya, axis=-1)
    y_pool = jnp.dot(ya, wbp_ref[...], preferred_element_type=F32)
    y_attn = jnp.dot(o_ref[0], wba_ref[...], preferred_element_type=F32)
    merged = (gt_ref[0, :, 0:d].astype(F32) * y_pool + gt_ref[0, :, d:2 * d].astype(F32) * y_attn)
    mix = jnp.dot(merged.astype(BF16), wout_ref[...], preferred_element_type=F32)
    x1 = x_ref[0] + mod_ref[0, 2:3, :] * mix
    x1_ref[0] = x1
    h2 = (_rms(x1) * gain_ref[...]) * (1.0 + mod_ref[0, 4:5, :]) + mod_ref[0, 3:4, :]
    h2_ref[0] = h2.astype(h2_ref.dtype)


def _merge_out(x, xp, o, gates, mod, gain_f, pool_w, pool_scale, wbp, wba, wout, tm):
    bsz, s, d = x.shape
    const2 = lambda b, i: (0, 0)
    tile = lambda b, i: (b, i, 0)
    hb = tm // POOL_HALO
    one = pl.Buffered(1)
    return pl.pallas_call(
        functools.partial(_merge_kernel, tm=tm),
        grid=(bsz, s // tm),
        in_specs=[pl.BlockSpec((1, tm, d), tile),
                  pl.BlockSpec((1, tm, d), tile),
                  pl.BlockSpec((1, POOL_HALO, d), lambda b, i: (b, jnp.maximum(i * hb - 1, 0), 0)),
                  pl.BlockSpec((1, tm, d), tile),
                  pl.BlockSpec((1, tm, 2 * d), tile),
                  pl.BlockSpec((1, N_MOD, d), lambda b, i: (b, 0, 0)),
                  pl.BlockSpec((1, d), const2),
                  pl.BlockSpec(pool_w.shape, lambda b, i: (0, 0, 0), pipeline_mode=one),
                  pl.BlockSpec((1, d), const2),
                  pl.BlockSpec(wbp.shape, const2, pipeline_mode=one),
                  pl.BlockSpec(wba.shape, const2, pipeline_mode=one),
                  pl.BlockSpec(wout.shape, const2, pipeline_mode=one)],
        out_specs=[pl.BlockSpec((1, tm, d), tile), pl.BlockSpec((1, tm, d), tile)],
        out_shape=[jax.ShapeDtypeStruct((bsz, s, d), F32),
                   jax.ShapeDtypeStruct((bsz, s, d), BF16)],
        scratch_shapes=[pltpu.VMEM((tm + POOL_HALO, d), F32)],
        compiler_params=pltpu.CompilerParams(
            dimension_semantics=("parallel", "parallel"), vmem_limit_bytes=VMEM_LIMIT),
        name="merge_out",
    )(x, xp, xp, o, gates, mod, gain_f, pool_w, pool_scale, wbp, wba, wout)


def _staircase():
    groups = []
    groups.append(([0] * 8, list(range(8))))
    groups.append(([0] * 8, list(range(8, 16))))
    for a in range(1, 8):
        nb = PEER_TOPK // (a + 1)
        groups.append(([a] * 8, [b if b < nb else -1 for b in range(8)]))
    groups.append((list(range(8, 16)), [0] * 8))
    return groups


def _extract_top(s, iota_f, n_rounds, big):
    vals, poss = [], []
    for _ in range(n_rounds):
        m = jnp.max(s, axis=0, keepdims=True)
        pos = jnp.min(jnp.where(s == m, iota_f, big), axis=0, keepdims=True)
        vals.append(m)
        poss.append(pos)
        s = jnp.where(iota_f == pos, -jnp.inf, s)
    return vals, poss


def _route_kernel(h2_ref, wq_ref, sk_ref, isel_ref, jsel_ref, g_ref, it_ref, jt_ref, gt_ref,
                  *, tb):
    h2 = h2_ref[...]
    key_iota = lax.broadcasted_iota(I32, (N_KEYS, LANES), 0).astype(F32)
    groups = _staircase()
    rowi = lax.broadcasted_iota(I32, (8, LANES), 0)

    def const_rows(vals):
        out = jnp.zeros((8, LANES), F32)
        for r, v in enumerate(vals):
            out = jnp.where(rowi == r, float(v), out)
        return out

    def head(hh, _):
        q = jnp.dot(h2, wq_ref[hh], preferred_element_type=F32).astype(BF16)
        for sl in range(tb // LANES):
            lanes = slice(sl * LANES, (sl + 1) * LANES)
            sv, si = [], []
            for half in range(2):
                qh = q[lanes, half * PEER_HALF:(half + 1) * PEER_HALF]
                st = lax.dot_general(sk_ref[half], qh, (((1,), (1,)), ((), ())),
                                     preferred_element_type=F32)
                vals, poss = _extract_top(st, key_iota, PEER_TOPK, float(N_KEYS))
                sv.append(jnp.concatenate(vals, axis=0))
                si.append(jnp.concatenate(poss, axis=0))
            cand, code = [], []
            for a_list, b_list in groups:
                if len(set(a_list)) == 1:
                    a = a_list[0]
                    b0 = b_list[0]
                    rows = sv[0][a:a + 1] + sv[1][b0:b0 + 8]
                else:
                    rows = sv[0][a_list[0]:a_list[0] + 8] + sv[1][0:1]
                valid = const_rows([1.0 if b >= 0 else 0.0 for b in b_list])
                cand.append(jnp.where(valid > 0.5, rows, -jnp.inf))
                code.append(const_rows([a * PEER_TOPK + max(b, 0)
                                        for a, b in zip(a_list, b_list)]))
            cand = jnp.concatenate(cand, axis=0)
            code = jnp.concatenate(code, axis=0)
            top_s, top_pos = _extract_top(cand, code, PEER_TOPK, 1e9)
            top_s = jnp.concatenate(top_s, axis=0)
            top_pos = jnp.concatenate(top_pos, axis=0)
            a_sel = jnp.floor(top_pos * (1.0 / PEER_TOPK))
            b_sel = top_pos - a_sel * PEER_TOPK
            i_sel = jnp.zeros_like(top_s)
            j_sel = jnp.zeros_like(top_s)
            for r in range(PEER_TOPK):
                i_sel = jnp.where(a_sel == float(r), si[0][r:r + 1], i_sel)
                j_sel = jnp.where(b_sel == float(r), si[1][r:r + 1], j_sel)
            e = jnp.exp(top_s - top_s[0:1])
            g = e / jnp.sum(e, axis=0, keepdims=True)
            rows = pl.ds(pl.multiple_of(hh * PEER_TOPK, PEER_TOPK), PEER_TOPK)
            it_ref[rows, lanes] = i_sel
            jt_ref[rows, lanes] = j_sel
            gt_ref[rows, lanes] = g
        return 0

    lax.fori_loop(0, PEER_HEADS, head, 0)
    isel_ref[...] = it_ref[...].T.astype(I32)
    jsel_ref[...] = jt_ref[...].T.astype(I32)
    g_ref[...] = gt_ref[...].T


def _peer_route(h2, wq, sk, tb):
    t, d = h2.shape
    tile = lambda i: (i, 0)
    one = pl.Buffered(1)
    return pl.pallas_call(
        functools.partial(_route_kernel, tb=tb),
        grid=(t // tb,),
        in_specs=[pl.BlockSpec((tb, d), tile),
                  pl.BlockSpec(wq.shape, lambda i: (0, 0, 0), pipeline_mode=one),
                  pl.BlockSpec(sk.shape, lambda i: (0, 0, 0), pipeline_mode=one)],
        out_specs=[pl.BlockSpec((tb, N_PAIRS), tile)] * 3,
        out_shape=[jax.ShapeDtypeStruct((t, N_PAIRS), I32),
                   jax.ShapeDtypeStruct((t, N_PAIRS), I32),
                   jax.ShapeDtypeStruct((t, N_PAIRS), F32)],
        scratch_shapes=[pltpu.VMEM((N_PAIRS, tb), F32)] * 3,
        compiler_params=pltpu.CompilerParams(
            dimension_semantics=("parallel",), vmem_limit_bytes=VMEM_LIMIT),
        name="peer_route",
    )(h2, wq, sk)


def _ffn_kernel(h2_ref, isel_ref, jsel_ref, g_ref, dnt_ref, up_ref, x1_ref, gate_ref, fg_ref,
                out_ref, w_ref, acc_ref, *, tb, ec):
    e = pl.program_id(1)

    @pl.when(e == 0)
    def _build():
        acc_ref[...] = jnp.zeros_like(acc_ref)
        key = lax.broadcasted_iota(I32, (N_KEYS, N_PAIRS), 0)

        def tok(t, _):
            ii = jnp.broadcast_to(isel_ref[pl.ds(t, 1), :], (N_KEYS, N_PAIRS))
            jj = jnp.broadcast_to(jsel_ref[pl.ds(t, 1), :], (N_KEYS, N_PAIRS))
            gg = jnp.broadcast_to(g_ref[pl.ds(t, 1), :], (N_KEYS, N_PAIRS))
            a = jnp.where(ii == key, 1.0, 0.0).astype(BF16)
            b = jnp.where(jj == key, gg, 0.0).astype(BF16)
            w = lax.dot_general(a, b, (((1,), (1,)), ((), ())), preferred_element_type=F32)
            w_ref[pl.ds(pl.multiple_of(t * W_PITCH, 8), N_KEYS), :] = w
            return 0

        lax.fori_loop(0, tb, tok, 0)

    h2 = h2_ref[...]
    n_sub = ec // MXU_DIM
    tiles_per_sub = MXU_DIM // LANES
    for c in range(n_sub):
        cols = slice(c * MXU_DIM, (c + 1) * MXU_DIM)
        pre = jnp.dot(h2, dnt_ref[:, cols], preferred_element_type=F32)
        i0 = e * (ec // LANES) + c * tiles_per_sub
        w = jnp.concatenate(
            [w_ref[pl.ds(i0 + u, tb, stride=W_PITCH), :] for u in range(tiles_per_sub)], axis=-1)
        z = (w * _gelu(pre)).astype(BF16)
        acc_ref[...] += jnp.dot(z, up_ref[cols, :], preferred_element_type=F32)

    @pl.when(e == pl.num_programs(1) - 1)
    def _finish():
        x2 = x1_ref[...] + gate_ref[0] * acc_ref[...]
        out_ref[...] = _rms(x2) * fg_ref[...]


def _peer_ffn(h2, isel, jsel, g, down_t, up, x1, gate_f, final_gain, tb, ec, tokens_per_batch):
    t, d = h2.shape
    n_exp = up.shape[0]
    tile = lambda i, e: (i, 0)
    tpb = tokens_per_batch // tb
    return pl.pallas_call(
        functools.partial(_ffn_kernel, tb=tb, ec=ec),
        grid=(t // tb, n_exp // ec),
        in_specs=[pl.BlockSpec((tb, d), tile),
                  pl.BlockSpec((tb, N_PAIRS), tile),
                  pl.BlockSpec((tb, N_PAIRS), tile),
                  pl.BlockSpec((tb, N_PAIRS), tile),
                  pl.BlockSpec((d, ec), lambda i, e: (0, e)),
                  pl.BlockSpec((ec, d), lambda i, e: (e, 0)),
                  pl.BlockSpec((tb, d), tile),
                  pl.BlockSpec((1, 1, d), lambda i, e: (i // tpb, 0, 0)),
                  pl.BlockSpec((1, d), lambda i, e: (0, 0))],
        out_specs=pl.BlockSpec((tb, d), tile),
        out_shape=jax.ShapeDtypeStruct((t, d), F32),
        scratch_shapes=[pltpu.VMEM((tb * W_PITCH, LANES), F32),
                        pltpu.VMEM((tb, d), F32)],
        compiler_params=pltpu.CompilerParams(
            dimension_semantics=("parallel", "arbitrary"), vmem_limit_bytes=VMEM_LIMIT),
        name="peer_ffn",
    )(h2, isel, jsel, g, down_t, up, x1, gate_f, final_gain)


def _layer(x, posf, invf, mod, l, p, tm, tq, tb_route, tb_ffn, ec):
    bsz, s, d = x.shape
    lam_init = 0.8 - 0.6 * math.exp(-0.3 * l)
    q, k, v, xp, gates = _mixer_in(
        x, posf, mod, p['norm_mix_gain'][l][None], invf, p['w_in'][l].astype(BF16),
        p['w_gate'][l].astype(BF16), p['b_gate'][l][None], tm)
    lam_vecs = jnp.stack([p['lambda_q1'][l], p['lambda_k1'][l],
                          p['lambda_q2'][l], p['lambda_k2'][l]])
    o = _diff_attn(lam_vecs, q, k, v, p['attn_subln_gain'][l][None], lam_init, tq)
    x1, h2 = _merge_out(
        x, xp, o, gates, mod, p['norm_ffn_gain'][l][None], p['pool_w'][l].astype(BF16),
        p['pool_scale'][l][None], p['w_branch_pool'][l].astype(BF16),
        p['w_branch_attn'][l].astype(BF16), p['w_out'][l].astype(BF16), tm)
    h2f = h2.reshape(bsz * s, d)
    wq = p['peer_w_query'][l].astype(BF16).reshape(d, PEER_HEADS, 2 * PEER_HALF).transpose(1, 0, 2)
    isel, jsel, g = _peer_route(h2f, wq, p['peer_sub_keys'][l].astype(BF16), tb_route)
    return h2f, isel, jsel, g, x1.reshape(bsz * s, d)


def kernel(x, c, positions, norm_mix_gain, norm_ffn_gain, w_ada, b_ada, w_in, w_gate, b_gate,
           pool_w, pool_scale, lambda_q1, lambda_k1, lambda_q2, lambda_k2, attn_subln_gain,
           w_branch_pool, w_branch_attn, w_out, peer_w_query, peer_sub_keys, peer_down, peer_up,
           final_norm_gain):
    bsz, s, d = x.shape
    depth = w_in.shape[0]
    assert depth == 1, "final RMSNorm is fused into the last layer's peer_ffn call"
    p = dict(norm_mix_gain=norm_mix_gain, norm_ffn_gain=norm_ffn_gain, w_in=w_in, w_gate=w_gate,
             b_gate=b_gate, pool_w=pool_w, pool_scale=pool_scale, lambda_q1=lambda_q1,
             lambda_k1=lambda_k1, lambda_q2=lambda_q2, lambda_k2=lambda_k2,
             attn_subln_gain=attn_subln_gain, w_branch_pool=w_branch_pool,
             w_branch_attn=w_branch_attn, w_out=w_out, peer_w_query=peer_w_query,
             peer_sub_keys=peer_sub_keys)
    tm = min(256, s)
    tq = min(256, s)
    tb_route = 256
    tb_ffn = 256
    ec = 2048
    inv_freq = ROPE_THETA ** (-jnp.arange(0, ROT_DIM, 2, dtype=F32) / ROT_DIM)
    invf = jnp.tile(inv_freq, LANES // (ROT_DIM // 2))[None]
    posf = positions.astype(F32)[..., None]
    l = 0
    mod = _ada_mod(c, w_ada[l], b_ada[l]).reshape(bsz, N_MOD, d)
    h2f, isel, jsel, g, x1f = _layer(x, posf, invf, mod, l, p, tm, tq, tb_route, tb_ffn, ec)
    down_t = peer_down[l].astype(BF16).T
    out = _peer_ffn(h2f, isel, jsel, g, down_t, peer_up[l].astype(BF16), x1f,
                    mod[:, 5:6, :], final_norm_gain[None], tb_ffn, ec, s)
    return out.reshape(bsz, s, d)
```

```python
import functools
import math

import jax
import jax.numpy as jnp
from jax import lax
from jax.experimental import pallas as pl
from jax.experimental.pallas import tpu as pltpu

F32 = jnp.float32
BF16 = jnp.bfloat16
I32 = jnp.int32

EPS = 1e-6
NEG_INF = -1e30

N_ATTN_HEADS = 8
ATTN_HEAD_DIM = 64
ATTN_V_DIM = 128
ROT_DIM = 16
ROPE_THETA = 500000.0
POOL_WINDOWS = (2, 4, 8, 16)
POOL_GROUP_DIM = 256
POOL_HALO = 16
N_MOD = 6

PEER_HEADS = 8
N_KEYS = 128
PEER_HALF = 128
PEER_TOPK = 16
N_PAIRS = PEER_HEADS * PEER_TOPK

LANES = 128
MXU_DIM = 256
VMEM_LIMIT = 56 * 1024 * 1024

W_PITCH = 72


def _rms(x, eps=EPS):
    return x * lax.rsqrt(jnp.mean(x * x, axis=-1, keepdims=True) + eps)


def _gelu(x):
    return 0.5 * x * (1.0 + lax.erf(x * math.sqrt(0.5)))


def _ada_kernel(c_ref, w_ref, b_ref, o_ref):
    ca = jax.nn.silu(c_ref[...])
    o_ref[...] = jnp.dot(ca, w_ref[...], preferred_element_type=F32,
                         precision=lax.Precision.HIGHEST) + b_ref[...]


def _ada_mod(c, w, b):
    bsz, d = c.shape
    n = w.shape[1]
    tn = 1536
    return pl.pallas_call(
        _ada_kernel,
        grid=(n // tn,),
        in_specs=[pl.BlockSpec((bsz, d), lambda j: (0, 0)),
                  pl.BlockSpec((d, tn), lambda j: (0, j)),
                  pl.BlockSpec((1, tn), lambda j: (0, j))],
        out_specs=pl.BlockSpec((bsz, tn), lambda j: (0, j)),
        out_shape=jax.ShapeDtypeStruct((bsz, n), F32),
        compiler_params=pltpu.CompilerParams(vmem_limit_bytes=VMEM_LIMIT),
        name="ada_mod",
    )(c, w, b.reshape(1, n))


def _mixer_in_kernel(x_ref, pos_ref, mod_ref, gain_ref, invf_ref, win_ref, wg_ref, bg_ref,
                     q_ref, k_ref, v_ref, xp_ref, gt_ref):
    d = x_ref.shape[-1]
    x = x_ref[0]
    shift = mod_ref[0, 0:1, :]
    scale = mod_ref[0, 1:2, :]
    h = (_rms(x) * gain_ref[...]) * (1.0 + scale) + shift
    hb = h.astype(BF16)

    ang = pos_ref[0] * invf_ref[...]
    cos = jnp.cos(ang)
    sin = jnp.sin(ang)
    lane = lax.broadcasted_iota(I32, (1, LANES), 1) % ATTN_HEAD_DIM
    half = ROT_DIM // 2
    cm = jnp.where(lane < ROT_DIM, cos, 1.0)
    sa = jnp.where(lane < half, -sin, 0.0)
    sb = jnp.where((lane >= half) & (lane < ROT_DIM), sin, 0.0)

    def rope(t):
        return (t * cm + pltpu.roll(t, LANES - half, 1) * sa + pltpu.roll(t, half, 1) * sb)

    qk_scale = ATTN_HEAD_DIM ** -0.5
    for c, (ref, scl) in enumerate(((q_ref, qk_scale), (k_ref, 1.0))):
        p = jnp.dot(hb, win_ref[:, c * d:(c + 1) * d], preferred_element_type=F32)
        for hh in range(d // LANES):
            sl = slice(hh * LANES, (hh + 1) * LANES)
            ref[0, :, sl] = (rope(p[:, sl]) * scl).astype(ref.dtype)
    v_ref[0] = jnp.dot(hb, win_ref[:, 2 * d:3 * d], preferred_element_type=F32).astype(v_ref.dtype)
    xp_ref[0] = jnp.dot(hb, win_ref[:, 3 * d:4 * d], preferred_element_type=F32)
    for c in range(2):
        g = jnp.dot(hb, wg_ref[:, c * d:(c + 1) * d], preferred_element_type=F32)
        gt_ref[0, :, c * d:(c + 1) * d] = jax.nn.sigmoid(
            g + bg_ref[:, c * d:(c + 1) * d]).astype(gt_ref.dtype)


def _mixer_in(x, posf, mod, gain, invf, w_in, w_gate, b_gate, tm):
    bsz, s, d = x.shape
    const = lambda b, i: (0, 0)
    tile = lambda b, i: (b, i, 0)
    one = pl.Buffered(1)
    return pl.pallas_call(
        _mixer_in_kernel,
        grid=(bsz, s // tm),
        in_specs=[pl.BlockSpec((1, tm, d), tile),
                  pl.BlockSpec((1, tm, 1), tile),
                  pl.BlockSpec((1, N_MOD, d), lambda b, i: (b, 0, 0)),
                  pl.BlockSpec((1, d), const),
                  pl.BlockSpec((1, LANES), const),
                  pl.BlockSpec(w_in.shape, const, pipeline_mode=one),
                  pl.BlockSpec(w_gate.shape, const, pipeline_mode=one),
                  pl.BlockSpec((1, 2 * d), const)],
        out_specs=[pl.BlockSpec((1, tm, d), tile)] * 4 + [pl.BlockSpec((1, tm, 2 * d), tile)],
        out_shape=[jax.ShapeDtypeStruct((bsz, s, d), BF16),
                   jax.ShapeDtypeStruct((bsz, s, d), BF16),
                   jax.ShapeDtypeStruct((bsz, s, d), BF16),
                   jax.ShapeDtypeStruct((bsz, s, d), F32),
                   jax.ShapeDtypeStruct((bsz, s, 2 * d), BF16)],
        compiler_params=pltpu.CompilerParams(
            dimension_semantics=("parallel", "parallel"), vmem_limit_bytes=VMEM_LIMIT),
        name="mixer_in",
    )(x, posf, mod, gain, invf, w_in, w_gate, b_gate)


def _attn_kernel(lam_ref, q_ref, k_ref, v_ref, gain_ref, o_ref, *, tq, lam_init):
    qi = pl.program_id(2)
    q = q_ref[0]
    lane = lax.broadcasted_iota(I32, q.shape, 1)
    zero = jnp.zeros_like(q)
    qq = jnp.concatenate([jnp.where(lane < ATTN_HEAD_DIM, q, zero),
                          jnp.where(lane >= ATTN_HEAD_DIM, q, zero)], axis=0)

    def step(j, carry, diagonal):
        m, l, acc = carry
        kb = k_ref[0, pl.ds(pl.multiple_of(j * tq, tq), tq), :]
        vb = v_ref[0, pl.ds(pl.multiple_of(j * tq, tq), tq), :]
        s = lax.dot_general(qq, kb, (((1,), (1,)), ((), ())), preferred_element_type=F32)
        if diagonal:
            row = lax.broadcasted_iota(I32, s.shape, 0) % tq
            col = lax.broadcasted_iota(I32, s.shape, 1)
            s = jnp.where(row >= col, s, NEG_INF)
        m_new = jnp.maximum(m, jnp.max(s, axis=-1, keepdims=True))
        alpha = jnp.exp(m - m_new)
        p = jnp.exp(s - m_new)
        l = alpha * l + jnp.sum(p, axis=-1, keepdims=True)
        acc = alpha * acc + jnp.dot(p.astype(vb.dtype), vb, preferred_element_type=F32)
        return m_new, l, acc

    init = (jnp.full((2 * tq, 1), NEG_INF, F32), jnp.zeros((2 * tq, 1), F32),
            jnp.zeros((2 * tq, ATTN_V_DIM), F32))
    carry = lax.fori_loop(0, qi, functools.partial(step, diagonal=False), init)
    _, l, acc = step(qi, carry, True)

    lv = lam_ref[...]
    lam = (jnp.exp(jnp.sum(lv[0:1] * lv[1:2], axis=-1, keepdims=True))
           - jnp.exp(jnp.sum(lv[2:3] * lv[3:4], axis=-1, keepdims=True)) + lam_init)
    o = acc[:tq] / l[:tq] - lam * (acc[tq:] / l[tq:])
    o_ref[0] = (_rms(o) * gain_ref[...] * (1.0 - lam_init)).astype(o_ref.dtype)


def _diff_attn(lam_vecs, q, k, v, gain, lam_init, tq):
    bsz, s, d = q.shape
    nh = d // LANES
    kv_spec = pl.BlockSpec((1, s, LANES), lambda b, h, i: (b, 0, h))
    return pl.pallas_call(
        functools.partial(_attn_kernel, tq=tq, lam_init=lam_init),
        grid=(bsz, nh, s // tq),
        in_specs=[pl.BlockSpec(lam_vecs.shape, lambda b, h, i: (0, 0)),
                  pl.BlockSpec((1, tq, LANES), lambda b, h, i: (b, i, h)),
                  kv_spec, kv_spec,
                  pl.BlockSpec((1, LANES), lambda b, h, i: (0, h))],
        out_specs=pl.BlockSpec((1, tq, LANES), lambda b, h, i: (b, i, h)),
        out_shape=jax.ShapeDtypeStruct((bsz, s, d), BF16),
        compiler_params=pltpu.CompilerParams(
            dimension_semantics=("parallel", "parallel", "parallel"),
            vmem_limit_bytes=VMEM_LIMIT),
        name="diff_attn",
    )(lam_vecs, q, k, v, gain)


def _merge_kernel(x_ref, xp_ref, halo_ref, o_ref, gt_ref, mod_ref, gain_ref, pw_ref, ps_ref,
                  wbp_ref, wba_ref, wout_ref, x1_ref, h2_ref, xs_ref, *, tm):
    i = pl.program_id(1)
    d = x_ref.shape[-1]
    halo = halo_ref[0]
    xs_ref[0:POOL_HALO, :] = jnp.where(i > 0, halo, jnp.zeros_like(halo))
    xs_ref[POOL_HALO:, :] = xp_ref[0]
    pos = i * tm + lax.broadcasted_iota(I32, (tm, 1), 0)
    y_pool = None
    ya = []
    for g, w in enumerate(POOL_WINDOWS):
        cols = slice(g * POOL_GROUP_DIM, (g + 1) * POOL_GROUP_DIM)
        tok = xs_ref[POOL_HALO:, cols]
        acc = tok
        for u in range(1, w):
            acc = acc + xs_ref[POOL_HALO - u:POOL_HALO - u + tm, cols]
        cnt = jnp.minimum(pos + 1, w).astype(F32)
        pooled = acc / cnt - tok
        yg = jnp.dot(pooled.astype(BF16), pw_ref[g], preferred_element_type=F32)
        ya.append((yg * ps_ref[:, cols]).astype(BF16))
    ya = jnp.concatenate(ya, axis=-1)
    y_pool = jnp.dot(ya, wbp_ref[...], preferred_element_type=F32)
    y_attn = jnp.dot(o_ref[0], wba_ref[...], preferred_element_type=F32)
    merged = (gt_ref[0, :, 0:d].astype(F32) * y_pool + gt_ref[0, :, d:2 * d].astype(F32) * y_attn)
    mix = jnp.dot(merged.astype(BF16), wout_ref[...], preferred_element_type=F32)
    x1 = x_ref[0] + mod_ref[0, 2:3, :] * mix
    x1_ref[0] = x1
    h2 = (_rms(x1) * gain_ref[...]) * (1.0 + mod_ref[0, 4:5, :]) + mod_ref[0, 3:4, :]
    h2_ref[0] = h2.astype(h2_ref.dtype)


def _merge_out(x, xp, o, gates, mod, gain_f, pool_w, pool_scale, wbp, wba, wout, tm):
    bsz, s, d = x.shape
    const2 = lambda b, i: (0, 0)
    tile = lambda b, i: (b, i, 0)
    hb = tm // POOL_HALO
    one = pl.Buffered(1)
    return pl.pallas_call(
        functools.partial(_merge_kernel, tm=tm),
        grid=(bsz, s // tm),
        in_specs=[pl.BlockSpec((1, tm, d), tile),
                  pl.BlockSpec((1, tm, d), tile),
                  pl.BlockSpec((1, POOL_HALO, d), lambda b, i: (b, jnp.maximum(i * hb - 1, 0), 0)),
                  pl.BlockSpec((1, tm, d), tile),
                  pl.BlockSpec((1, tm, 2 * d), tile),
                  pl.BlockSpec((1, N_MOD, d), lambda b, i: (b, 0, 0)),
                  pl.BlockSpec((1, d), const2),
                  pl.BlockSpec(pool_w.shape, lambda b, i: (0, 0, 0), pipeline_mode=one),
                  pl.BlockSpec((1, d), const2),
                  pl.BlockSpec(wbp.shape, const2, pipeline_mode=one),
                  pl.BlockSpec(wba.shape, const2, pipeline_mode=one),
                  pl.BlockSpec(wout.shape, const2, pipeline_mode=one)],
        out_specs=[pl.BlockSpec((1, tm, d), tile), pl.BlockSpec((1, tm, d), tile)],
        out_shape=[jax.ShapeDtypeStruct((bsz, s, d), F32),
                   jax.ShapeDtypeStruct((bsz, s, d), BF16)],
        scratch_shapes=[pltpu.VMEM((tm + POOL_HALO, d), F32)],
        compiler_params=pltpu.CompilerParams(
            dimension_semantics=("parallel", "parallel"), vmem_limit_bytes=VMEM_LIMIT),
        name="merge_out",
    )(x, xp, xp, o, gates, mod, gain_f, pool_w, pool_scale, wbp, wba, wout)


def _staircase():
    groups = []
    groups.append(([0] * 8, list(range(8))))
    groups.append(([0] * 8, list(range(8, 16))))
    for a in range(1, 8):
        nb = PEER_TOPK // (a + 1)
        groups.append(([a] * 8, [b if b < nb else -1 for b in range(8)]))
    groups.append((list(range(8, 16)), [0] * 8))
    return groups


def _extract_top(s, iota_f, n_rounds, big):
    vals, poss = [], []
    for _ in range(n_rounds):
        m = jnp.max(s, axis=0, keepdims=True)
        pos = jnp.min(jnp.where(s == m, iota_f, big), axis=0, keepdims=True)
        vals.append(m)
        poss.append(pos)
        s = jnp.where(iota_f == pos, -jnp.inf, s)
    return vals, poss


def _route_kernel(h2_ref, wq_ref, sk_ref, isel_ref, jsel_ref, g_ref, it_ref, jt_ref, gt_ref,
                  *, tb):
    h2 = h2_ref[...]
    key_iota = lax.broadcasted_iota(I32, (N_KEYS, LANES), 0).astype(F32)
    groups = _staircase()
    rowi = lax.broadcasted_iota(I32, (8, LANES), 0)

    def const_rows(vals):
        out = jnp.zeros((8, LANES), F32)
        for r, v in enumerate(vals):
            out = jnp.where(rowi == r, float(v), out)
        return out

    def head(hh, _):
        q = jnp.dot(h2, wq_ref[hh], preferred_element_type=F32).astype(BF16)
        for sl in range(tb // LANES):
            lanes = slice(sl * LANES, (sl + 1) * LANES)
            sv, si = [], []
            for half in range(2):
                qh = q[lanes, half * PEER_HALF:(half + 1) * PEER_HALF]
                st = lax.dot_general(sk_ref[half], qh, (((1,), (1,)), ((), ())),
                                     preferred_element_type=F32)
                vals, poss = _extract_top(st, key_iota, PEER_TOPK, float(N_KEYS))
                sv.append(jnp.concatenate(vals, axis=0))
                si.append(jnp.concatenate(poss, axis=0))
            cand, code = [], []
            for a_list, b_list in groups:
                if len(set(a_list)) == 1:
                    a = a_list[0]
                    b0 = b_list[0]
                    rows = sv[0][a:a + 1] + sv[1][b0:b0 + 8]
                else:
                    rows = sv[0][a_list[0]:a_list[0] + 8] + sv[1][0:1]
                valid = const_rows([1.0 if b >= 0 else 0.0 for b in b_list])
                cand.append(jnp.where(valid > 0.5, rows, -jnp.inf))
                code.append(const_rows([a * PEER_TOPK + max(b, 0)
                                        for a, b in zip(a_list, b_list)]))
            cand = jnp.concatenate(cand, axis=0)
            code = jnp.concatenate(code, axis=0)
            top_s, top_pos = _extract_top(cand, code, PEER_TOPK, 1e9)
            top_s = jnp.concatenate(top_s, axis=0)
            top_pos = jnp.concatenate(top_pos, axis=0)
            a_sel = jnp.floor(top_pos * (1.0 / PEER_TOPK))
            b_sel = top_pos - a_sel * PEER_TOPK
            i_sel = jnp.zeros_like(top_s)
            j_sel = jnp.zeros_like(top_s)
            for r in range(PEER_TOPK):
                i_sel = jnp.where(a_sel == float(r), si[0][r:r + 1], i_sel)
                j_sel = jnp.where(b_sel == float(r), si[1][r:r + 1], j_sel)
            e = jnp.exp(top_s - top_s[0:1])
            g = e / jnp.sum(e, axis=0, keepdims=True)
            rows = pl.ds(pl.multiple_of(hh * PEER_TOPK, PEER_TOPK), PEER_TOPK)
            it_ref[rows, lanes] = i_sel
            jt_ref[rows, lanes] = j_sel
            gt_ref[rows, lanes] = g
        return 0

    lax.fori_loop(0, PEER_HEADS, head, 0)
    isel_ref[...] = it_ref[...].T.astype(I32)
    jsel_ref[...] = jt_ref[...].T.astype(I32)
    g_ref[...] = gt_ref[...].T


def _peer_route(h2, wq, sk, tb):
    t, d = h2.shape
    tile = lambda i: (i, 0)
    one = pl.Buffered(1)
    return pl.pallas_call(
        functools.partial(_route_kernel, tb=tb),
        grid=(t // tb,),
        in_specs=[pl.BlockSpec((tb, d), tile),
                  pl.BlockSpec(wq.shape, lambda i: (0, 0, 0), pipeline_mode=one),
                  pl.BlockSpec(sk.shape, lambda i: (0, 0, 0), pipeline_mode=one)],
        out_specs=[pl.BlockSpec((tb, N_PAIRS), tile)] * 3,
        out_shape=[jax.ShapeDtypeStruct((t, N_PAIRS), I32),
                   jax.ShapeDtypeStruct((t, N_PAIRS), I32),
                   jax.ShapeDtypeStruct((t, N_PAIRS), F32)],
        scratch_shapes=[pltpu.VMEM((N_PAIRS, tb), F32)] * 3,
        compiler_params=pltpu.CompilerParams(
            dimension_semantics=("parallel",), vmem_limit_bytes=VMEM_LIMIT),
        name="peer_route",
    )(h2, wq, sk)


def _ffn_kernel(h2_ref, isel_ref, jsel_ref, g_ref, dnt_ref, up_ref, x1_ref, gate_ref, fg_ref,
                out_ref, w_ref, acc_ref, *, tb, ec):
    e = pl.program_id(1)
    hi_mask = jnp.uint32(0xFFFF0000)

    @pl.when(e == 0)
    def _build():
        acc_ref[...] = jnp.zeros_like(acc_ref)
        r = lax.broadcasted_iota(I32, (N_KEYS, N_PAIRS), 0)
        key_i = 2 * (8 * (r // 16) + r % 8) + (r // 8) % 2
        key_j = r

        def tok(t, _):
            ii = jnp.broadcast_to(isel_ref[pl.ds(t, 1), :], (N_KEYS, N_PAIRS))
            jj = jnp.broadcast_to(jsel_ref[pl.ds(t, 1), :], (N_KEYS, N_PAIRS))
            gg = jnp.broadcast_to(g_ref[pl.ds(t, 1), :], (N_KEYS, N_PAIRS))
            a = jnp.where(ii == key_i, 1.0, 0.0).astype(BF16)
            b = jnp.where(jj == key_j, gg, 0.0).astype(BF16)
            w = lax.dot_general(a, b, (((1,), (1,)), ((), ())), preferred_element_type=F32)
            w = w.astype(BF16).astype(F32).reshape(N_KEYS // 16, 16, LANES)
            lo = lax.bitcast_convert_type(w[:, 0:8, :], jnp.uint32) >> 16
            hi = lax.bitcast_convert_type(w[:, 8:16, :], jnp.uint32) & hi_mask
            w_ref[pl.ds(pl.multiple_of(t * W_PITCH, 8), N_KEYS // 2), :] = (
                (lo | hi).reshape(N_KEYS // 2, LANES))
            return 0

        lax.fori_loop(0, tb, tok, 0, unroll=8)

    h2 = h2_ref[...]
    for c in range(ec // MXU_DIM):
        cols = slice(c * MXU_DIM, (c + 1) * MXU_DIM)
        pre = jnp.dot(h2, dnt_ref[:, cols], preferred_element_type=F32)
        q = e * (ec // MXU_DIM) + c
        packed = w_ref[pl.ds(q, tb, stride=W_PITCH), :]
        w = jnp.concatenate([lax.bitcast_convert_type(packed << 16, F32),
                             lax.bitcast_convert_type(packed & hi_mask, F32)], axis=-1)
        z = (w * _gelu(pre)).astype(BF16)
        acc_ref[...] += jnp.dot(z, up_ref[cols, :], preferred_element_type=F32)

    @pl.when(e == pl.num_programs(1) - 1)
    def _finish():
        x2 = x1_ref[...] + gate_ref[0] * acc_ref[...]
        out_ref[...] = _rms(x2) * fg_ref[...]


def _peer_ffn(h2, isel, jsel, g, down_t, up, x1, gate_f, final_gain, tb, ec, tokens_per_batch):
    t, d = h2.shape
    n_exp = up.shape[0]
    tile = lambda i, e: (i, 0)
    tpb = tokens_per_batch // tb
    return pl.pallas_call(
        functools.partial(_ffn_kernel, tb=tb, ec=ec),
        grid=(t // tb, n_exp // ec),
        in_specs=[pl.BlockSpec((tb, d), tile),
                  pl.BlockSpec((tb, N_PAIRS), tile),
                  pl.BlockSpec((tb, N_PAIRS), tile),
                  pl.BlockSpec((tb, N_PAIRS), tile),
                  pl.BlockSpec((d, ec), lambda i, e: (0, e)),
                  pl.BlockSpec((ec, d), lambda i, e: (e, 0)),
                  pl.BlockSpec((tb, d), tile),
                  pl.BlockSpec((1, 1, d), lambda i, e: (i // tpb, 0, 0)),
                  pl.BlockSpec((1, d), lambda i, e: (0, 0))],
        out_specs=pl.BlockSpec((tb, d), tile),
        out_shape=jax.ShapeDtypeStruct((t, d), F32),
        scratch_shapes=[pltpu.VMEM((tb * W_PITCH, LANES), jnp.uint32),
                        pltpu.VMEM((tb, d), F32)],
        compiler_params=pltpu.CompilerParams(
            dimension_semantics=("parallel", "arbitrary"), vmem_limit_bytes=VMEM_LIMIT),
        name="peer_ffn",
    )(h2, isel, jsel, g, down_t, up, x1, gate_f, final_gain)


def _layer(x, posf, invf, mod, l, p, tm, tq, tb_route, tb_ffn, ec):
    bsz, s, d = x.shape
    lam_init = 0.8 - 0.6 * math.exp(-0.3 * l)
    q, k, v, xp, gates = _mixer_in(
        x, posf, mod, p['norm_mix_gain'][l][None], invf, p['w_in'][l].astype(BF16),
        p['w_gate'][l].astype(BF16), p['b_gate'][l][None], tm)
    lam_vecs = jnp.stack([p['lambda_q1'][l], p['lambda_k1'][l],
                          p['lambda_q2'][l], p['lambda_k2'][l]])
    o = _diff_attn(lam_vecs, q, k, v, p['attn_subln_gain'][l][None], lam_init, tq)
    x1, h2 = _merge_out(
        x, xp, o, gates, mod, p['norm_ffn_gain'][l][None], p['pool_w'][l].astype(BF16),
        p['pool_scale'][l][None], p['w_branch_pool'][l].astype(BF16),
        p['w_branch_attn'][l].astype(BF16), p['w_out'][l].astype(BF16), tm)
    h2f = h2.reshape(bsz * s, d)
    wq = p['peer_w_query'][l].astype(BF16).reshape(d, PEER_HEADS, 2 * PEER_HALF).transpose(1, 0, 2)
    isel, jsel, g = _peer_route(h2f, wq, p['peer_sub_keys'][l].astype(BF16), tb_route)
    return h2f, isel, jsel, g, x1.reshape(bsz * s, d)


def kernel(x, c, positions, norm_mix_gain, norm_ffn_gain, w_ada, b_ada, w_in, w_gate, b_gate,
           pool_w, pool_scale, lambda_q1, lambda_k1, lambda_q2, lambda_k2, attn_subln_gain,
           w_branch_pool, w_branch_attn, w_out, peer_w_query, peer_sub_keys, peer_down, peer_up,
           final_norm_gain):
    bsz, s, d = x.shape
    depth = w_in.shape[0]
    assert depth == 1, "final RMSNorm is fused into the last layer's peer_ffn call"
    p = dict(norm_mix_gain=norm_mix_gain, norm_ffn_gain=norm_ffn_gain, w_in=w_in, w_gate=w_gate,
             b_gate=b_gate, pool_w=pool_w, pool_scale=pool_scale, lambda_q1=lambda_q1,
             lambda_k1=lambda_k1, lambda_q2=lambda_q2, lambda_k2=lambda_k2,
             attn_subln_gain=attn_subln_gain, w_branch_pool=w_branch_pool,
             w_branch_attn=w_branch_attn, w_out=w_out, peer_w_query=peer_w_query,
             peer_sub_keys=peer_sub_keys)
    tm = min(256, s)
    tq = min(256, s)
    tb_route = 256
    tb_ffn = 512
    ec = 2048
    inv_freq = ROPE_THETA ** (-jnp.arange(0, ROT_DIM, 2, dtype=F32) / ROT_DIM)
    invf = jnp.tile(inv_freq, LANES // (ROT_DIM // 2))[None]
    posf = positions.astype(F32)[..., None]
    l = 0
    mod = _ada_mod(c, w_ada[l], b_ada[l]).reshape(bsz, N_MOD, d)
    h2f, isel, jsel, g, x1f = _layer(x, posf, invf, mod, l, p, tm, tq, tb_route, tb_ffn, ec)
    down_t = peer_down[l].astype(BF16).T
    out = _peer_ffn(h2f, isel, jsel, g, down_t, peer_up[l].astype(BF16), x1f,
                    mod[:, 5:6, :], final_norm_gain[None], tb_ffn, ec, s)
    return out.reshape(bsz, s, d)
```

```python
import functools
import math

import jax
import jax.numpy as jnp
from jax import lax
from jax.experimental import pallas as pl
from jax.experimental.pallas import tpu as pltpu

F32 = jnp.float32
BF16 = jnp.bfloat16
I32 = jnp.int32

EPS = 1e-6
NEG_INF = -1e30

N_ATTN_HEADS = 8
ATTN_HEAD_DIM = 64
ATTN_V_DIM = 128
ROT_DIM = 16
ROPE_THETA = 500000.0
POOL_WINDOWS = (2, 4, 8, 16)
POOL_GROUP_DIM = 256
POOL_HALO = 16
N_MOD = 6

PEER_HEADS = 8
N_KEYS = 128
PEER_HALF = 128
PEER_TOPK = 16
N_PAIRS = PEER_HEADS * PEER_TOPK

LANES = 128
MXU_DIM = 256
VMEM_LIMIT = 56 * 1024 * 1024

W_PITCH = 136


def _rms(x, eps=EPS):
    return x * lax.rsqrt(jnp.mean(x * x, axis=-1, keepdims=True) + eps)


def _gelu(x):
    return 0.5 * x * (1.0 + lax.erf(x * math.sqrt(0.5)))


def _ada_kernel(c_ref, w_ref, b_ref, o_ref):
    ca = jax.nn.silu(c_ref[...])
    o_ref[...] = jnp.dot(ca, w_ref[...], preferred_element_type=F32,
                         precision=lax.Precision.HIGHEST) + b_ref[...]


def _ada_mod(c, w, b):
    bsz, d = c.shape
    n = w.shape[1]
    tn = 1536
    return pl.pallas_call(
        _ada_kernel,
        grid=(n // tn,),
        in_specs=[pl.BlockSpec((bsz, d), lambda j: (0, 0)),
                  pl.BlockSpec((d, tn), lambda j: (0, j)),
                  pl.BlockSpec((1, tn), lambda j: (0, j))],
        out_specs=pl.BlockSpec((bsz, tn), lambda j: (0, j)),
        out_shape=jax.ShapeDtypeStruct((bsz, n), F32),
        compiler_params=pltpu.CompilerParams(vmem_limit_bytes=VMEM_LIMIT),
        name="ada_mod",
    )(c, w, b.reshape(1, n))


def _mixer_in_kernel(x_ref, pos_ref, mod_ref, gain_ref, invf_ref, win_ref, wg_ref, bg_ref,
                     qt_ref, k_ref, vt_ref, xp_ref, gt_ref):
    d = x_ref.shape[-1]
    x = x_ref[0]
    shift = mod_ref[0, 0:1, :]
    scale = mod_ref[0, 1:2, :]
    h = (_rms(x) * gain_ref[...]) * (1.0 + scale) + shift
    hb = h.astype(BF16)

    ang = pos_ref[0] * invf_ref[...]
    cos = jnp.cos(ang)
    sin = jnp.sin(ang)
    lane = lax.broadcasted_iota(I32, (1, LANES), 1) % ATTN_HEAD_DIM
    half = ROT_DIM // 2
    cm = jnp.where(lane < ROT_DIM, cos, 1.0)
    sa = jnp.where(lane < half, -sin, 0.0)
    sb = jnp.where((lane >= half) & (lane < ROT_DIM), sin, 0.0)

    def rope(t):
        return (t * cm + pltpu.roll(t, LANES - half, 1) * sa + pltpu.roll(t, half, 1) * sb)

    p = jnp.dot(hb, win_ref[:, 0:d], preferred_element_type=F32)
    for hh in range(d // LANES):
        sl = slice(hh * LANES, (hh + 1) * LANES)
        qt_ref[0, sl, :] = (rope(p[:, sl]) * ATTN_HEAD_DIM ** -0.5).T.astype(qt_ref.dtype)
    p = jnp.dot(hb, win_ref[:, d:2 * d], preferred_element_type=F32)
    for hh in range(d // LANES):
        sl = slice(hh * LANES, (hh + 1) * LANES)
        k_ref[0, :, sl] = rope(p[:, sl]).astype(k_ref.dtype)
    p = jnp.dot(hb, win_ref[:, 2 * d:3 * d], preferred_element_type=F32)
    for hh in range(d // LANES):
        sl = slice(hh * LANES, (hh + 1) * LANES)
        vt_ref[0, sl, :] = p[:, sl].T.astype(vt_ref.dtype)
    xp_ref[0] = jnp.dot(hb, win_ref[:, 3 * d:4 * d], preferred_element_type=F32)
    for c in range(2):
        g = jnp.dot(hb, wg_ref[:, c * d:(c + 1) * d], preferred_element_type=F32)
        gt_ref[0, :, c * d:(c + 1) * d] = jax.nn.sigmoid(
            g + bg_ref[:, c * d:(c + 1) * d]).astype(gt_ref.dtype)


def _mixer_in(x, posf, mod, gain, invf, w_in, w_gate, b_gate, tm):
    bsz, s, d = x.shape
    const = lambda b, i: (0, 0)
    tile = lambda b, i: (b, i, 0)
    ttile = lambda b, i: (b, 0, i)
    one = pl.Buffered(1)
    return pl.pallas_call(
        _mixer_in_kernel,
        grid=(bsz, s // tm),
        in_specs=[pl.BlockSpec((1, tm, d), tile),
                  pl.BlockSpec((1, tm, 1), tile),
                  pl.BlockSpec((1, N_MOD, d), lambda b, i: (b, 0, 0)),
                  pl.BlockSpec((1, d), const),
                  pl.BlockSpec((1, LANES), const),
                  pl.BlockSpec(w_in.shape, const, pipeline_mode=one),
                  pl.BlockSpec(w_gate.shape, const, pipeline_mode=one),
                  pl.BlockSpec((1, 2 * d), const)],
        out_specs=[pl.BlockSpec((1, d, tm), ttile), pl.BlockSpec((1, tm, d), tile),
                   pl.BlockSpec((1, d, tm), ttile), pl.BlockSpec((1, tm, d), tile),
                   pl.BlockSpec((1, tm, 2 * d), tile)],
        out_shape=[jax.ShapeDtypeStruct((bsz, d, s), BF16),
                   jax.ShapeDtypeStruct((bsz, s, d), BF16),
                   jax.ShapeDtypeStruct((bsz, d, s), BF16),
                   jax.ShapeDtypeStruct((bsz, s, d), F32),
                   jax.ShapeDtypeStruct((bsz, s, 2 * d), BF16)],
        compiler_params=pltpu.CompilerParams(
            dimension_semantics=("parallel", "parallel"), vmem_limit_bytes=VMEM_LIMIT),
        name="mixer_in",
    )(x, posf, mod, gain, invf, w_in, w_gate, b_gate)


def _attn_kernel(lam_ref, qt_ref, k_ref, vt_ref, gain_ref, o_ref, *, nblk, tq, hps, lam_init):
    sub = lax.broadcasted_iota(I32, (LANES, tq), 0)
    zero = jnp.zeros((LANES, tq), BF16)
    qq = []
    for h in range(hps):
        qt = qt_ref[0, h * LANES:(h + 1) * LANES, :]
        qq.append(jnp.concatenate([jnp.where(sub < ATTN_HEAD_DIM, qt, zero),
                                   jnp.where(sub >= ATTN_HEAD_DIM, qt, zero)], axis=1))
    m = [jnp.full((1, 2 * tq), NEG_INF, F32)] * hps
    l = [jnp.zeros((1, 2 * tq), F32)] * hps
    acc = [jnp.zeros((ATTN_V_DIM, 2 * tq), F32)] * hps

    def scores(j, h):
        return jnp.dot(k_ref[0, j * tq:(j + 1) * tq, h * LANES:(h + 1) * LANES], qq[h],
                       preferred_element_type=F32)

    s_next = [scores(0, h) for h in range(hps)]
    for j in range(nblk):
        for h in range(hps):
            s = s_next[h]
            if j + 1 < nblk:
                s_next[h] = scores(j + 1, h)
            if j == nblk - 1:
                kv = lax.broadcasted_iota(I32, s.shape, 0)
                qr = lax.broadcasted_iota(I32, s.shape, 1) % tq
                s = jnp.where(qr >= kv, s, NEG_INF)
            m_new = jnp.maximum(m[h], jnp.max(s, axis=0, keepdims=True))
            alpha = jnp.exp(m[h] - m_new)
            p = jnp.exp(s - m_new)
            l[h] = alpha * l[h] + jnp.sum(p, axis=0, keepdims=True)
            vt = vt_ref[0, h * LANES:(h + 1) * LANES, j * tq:(j + 1) * tq]
            acc[h] = alpha * acc[h] + jnp.dot(vt, p.astype(BF16),
                                              preferred_element_type=F32)
            m[h] = m_new

    lv = lam_ref[...]
    lam = (jnp.exp(jnp.sum(lv[0:1] * lv[1:2], axis=-1, keepdims=True))
           - jnp.exp(jnp.sum(lv[2:3] * lv[3:4], axis=-1, keepdims=True)) + lam_init)
    for h in range(hps):
        cols = slice(h * LANES, (h + 1) * LANES)
        ot = acc[h] / l[h]
        o = (ot[:, :tq] - lam * ot[:, tq:]).T
        o_ref[0, :, cols] = (_rms(o) * gain_ref[:, cols] * (1.0 - lam_init)).astype(o_ref.dtype)


def _diff_attn(lam_vecs, qt, k, vt, gain, lam_init, tq, hps):
    bsz, s, d = k.shape
    w = hps * LANES
    outs = []
    for qi in range(s // tq):
        kv_len = (qi + 1) * tq
        outs.append(pl.pallas_call(
            functools.partial(_attn_kernel, nblk=qi + 1, tq=tq, hps=hps, lam_init=lam_init),
            grid=(bsz, d // w),
            in_specs=[pl.BlockSpec(lam_vecs.shape, lambda b, h: (0, 0)),
                      pl.BlockSpec((1, w, tq), lambda b, h, qi=qi: (b, h, qi)),
                      pl.BlockSpec((1, kv_len, w), lambda b, h: (b, 0, h)),
                      pl.BlockSpec((1, w, kv_len), lambda b, h: (b, h, 0)),
                      pl.BlockSpec((1, w), lambda b, h: (0, h))],
            out_specs=pl.BlockSpec((1, tq, w), lambda b, h: (b, 0, h)),
            out_shape=jax.ShapeDtypeStruct((bsz, tq, d), BF16),
            compiler_params=pltpu.CompilerParams(
                dimension_semantics=("parallel", "parallel"), vmem_limit_bytes=VMEM_LIMIT),
            name=f"diff_attn_q{qi}",
        )(lam_vecs, qt, k, vt, gain))
    return jnp.concatenate(outs, axis=1)


def _merge_kernel(x_ref, xp_ref, halo_ref, o_ref, gt_ref, mod_ref, gain_ref, pw_ref, ps_ref,
                  wbp_ref, wba_ref, wout_ref, x1_ref, h2_ref, xs_ref, *, tm):
    i = pl.program_id(1)
    d = x_ref.shape[-1]
    halo = halo_ref[0]
    xs_ref[0:POOL_HALO, :] = jnp.where(i > 0, halo, jnp.zeros_like(halo))
    xs_ref[POOL_HALO:, :] = xp_ref[0]
    pos = i * tm + lax.broadcasted_iota(I32, (tm, 1), 0)
    y_pool = None
    ya = []
    for g, w in enumerate(POOL_WINDOWS):
        cols = slice(g * POOL_GROUP_DIM, (g + 1) * POOL_GROUP_DIM)
        tok = xs_ref[POOL_HALO:, cols]
        acc = tok
        for u in range(1, w):
            acc = acc + xs_ref[POOL_HALO - u:POOL_HALO - u + tm, cols]
        cnt = jnp.minimum(pos + 1, w).astype(F32)
        pooled = acc / cnt - tok
        yg = jnp.dot(pooled.astype(BF16), pw_ref[g], preferred_element_type=F32)
        ya.append((yg * ps_ref[:, cols]).astype(BF16))
    ya = jnp.concatenate(ya, axis=-1)
    y_pool = jnp.dot(ya, wbp_ref[...], preferred_element_type=F32)
    y_attn = jnp.dot(o_ref[0], wba_ref[...], preferred_element_type=F32)
    merged = (gt_ref[0, :, 0:d].astype(F32) * y_pool + gt_ref[0, :, d:2 * d].astype(F32) * y_attn)
    mix = jnp.dot(merged.astype(BF16), wout_ref[...], preferred_element_type=F32)
    x1 = x_ref[0] + mod_ref[0, 2:3, :] * mix
    x1_ref[0] = x1
    h2 = (_rms(x1) * gain_ref[...]) * (1.0 + mod_ref[0, 4:5, :]) + mod_ref[0, 3:4, :]
    h2_ref[0] = h2.astype(h2_ref.dtype)


def _merge_out(x, xp, o, gates, mod, gain_f, pool_w, pool_scale, wbp, wba, wout, tm):
    bsz, s, d = x.shape
    const2 = lambda b, i: (0, 0)
    tile = lambda b, i: (b, i, 0)
    hb = tm // POOL_HALO
    one = pl.Buffered(1)
    return pl.pallas_call(
        functools.partial(_merge_kernel, tm=tm),
        grid=(bsz, s // tm),
        in_specs=[pl.BlockSpec((1, tm, d), tile),
                  pl.BlockSpec((1, tm, d), tile),
                  pl.BlockSpec((1, POOL_HALO, d), lambda b, i: (b, jnp.maximum(i * hb - 1, 0), 0)),
                  pl.BlockSpec((1, tm, d), tile),
                  pl.BlockSpec((1, tm, 2 * d), tile),
                  pl.BlockSpec((1, N_MOD, d), lambda b, i: (b, 0, 0)),
                  pl.BlockSpec((1, d), const2),
                  pl.BlockSpec(pool_w.shape, lambda b, i: (0, 0, 0), pipeline_mode=one),
                  pl.BlockSpec((1, d), const2),
                  pl.BlockSpec(wbp.shape, const2, pipeline_mode=one),
                  pl.BlockSpec(wba.shape, const2, pipeline_mode=one),
                  pl.BlockSpec(wout.shape, const2, pipeline_mode=one)],
        out_specs=[pl.BlockSpec((1, tm, d), tile), pl.BlockSpec((1, tm, d), tile)],
        out_shape=[jax.ShapeDtypeStruct((bsz, s, d), F32),
                   jax.ShapeDtypeStruct((bsz, s, d), BF16)],
        scratch_shapes=[pltpu.VMEM((tm + POOL_HALO, d), F32)],
        compiler_params=pltpu.CompilerParams(
            dimension_semantics=("parallel", "parallel"), vmem_limit_bytes=VMEM_LIMIT),
        name="merge_out",
    )(x, xp, xp, o, gates, mod, gain_f, pool_w, pool_scale, wbp, wba, wout)


def _staircase():
    groups = []
    groups.append(([0] * 8, list(range(8))))
    groups.append(([0] * 8, list(range(8, 16))))
    for a in range(1, 8):
        nb = PEER_TOPK // (a + 1)
        groups.append(([a] * 8, [b if b < nb else -1 for b in range(8)]))
    groups.append((list(range(8, 16)), [0] * 8))
    return groups


def _extract_top(s, iota_f, n_rounds, big):
    vals, poss = [], []
    for _ in range(n_rounds):
        m = jnp.max(s, axis=0, keepdims=True)
        pos = jnp.min(jnp.where(s == m, iota_f, big), axis=0, keepdims=True)
        vals.append(m)
        poss.append(pos)
        s = jnp.where(iota_f == pos, -jnp.inf, s)
    return vals, poss


def _route_kernel(h2_ref, wq_ref, sk_ref, isel_ref, jsel_ref, g_ref, it_ref, jt_ref, gt_ref,
                  *, tb):
    h2 = h2_ref[...]
    key_iota = lax.broadcasted_iota(I32, (N_KEYS, LANES), 0).astype(F32)
    groups = _staircase()
    rowi = lax.broadcasted_iota(I32, (8, LANES), 0)

    def const_rows(vals):
        out = jnp.zeros((8, LANES), F32)
        for r, v in enumerate(vals):
            out = jnp.where(rowi == r, float(v), out)
        return out

    def head(hh, _):
        q = jnp.dot(h2, wq_ref[hh], preferred_element_type=F32).astype(BF16)
        for sl in range(tb // LANES):
            lanes = slice(sl * LANES, (sl + 1) * LANES)
            sv, si = [], []
            for half in range(2):
                qh = q[lanes, half * PEER_HALF:(half + 1) * PEER_HALF]
                st = lax.dot_general(sk_ref[half], qh, (((1,), (1,)), ((), ())),
                                     preferred_element_type=F32)
                vals, poss = _extract_top(st, key_iota, PEER_TOPK, float(N_KEYS))
                sv.append(jnp.concatenate(vals, axis=0))
                si.append(jnp.concatenate(poss, axis=0))
            cand, code = [], []
            for a_list, b_list in groups:
                if len(set(a_list)) == 1:
                    a = a_list[0]
                    b0 = b_list[0]
                    rows = sv[0][a:a + 1] + sv[1][b0:b0 + 8]
                else:
                    rows = sv[0][a_list[0]:a_list[0] + 8] + sv[1][0:1]
                valid = const_rows([1.0 if b >= 0 else 0.0 for b in b_list])
                cand.append(jnp.where(valid > 0.5, rows, -jnp.inf))
                code.append(const_rows([a * PEER_TOPK + max(b, 0)
                                        for a, b in zip(a_list, b_list)]))
            cand = jnp.concatenate(cand, axis=0)
            code = jnp.concatenate(code, axis=0)
            top_s, top_pos = _extract_top(cand, code, PEER_TOPK, 1e9)
            top_s = jnp.concatenate(top_s, axis=0)
            top_pos = jnp.concatenate(top_pos, axis=0)
            a_sel = jnp.floor(top_pos * (1.0 / PEER_TOPK))
            b_sel = top_pos - a_sel * PEER_TOPK
            i_sel = jnp.zeros_like(top_s)
            j_sel = jnp.zeros_like(top_s)
            for r in range(PEER_TOPK):
                i_sel = jnp.where(a_sel == float(r), si[0][r:r + 1], i_sel)
                j_sel = jnp.where(b_sel == float(r), si[1][r:r + 1], j_sel)
            e = jnp.exp(top_s - top_s[0:1])
            g = e / jnp.sum(e, axis=0, keepdims=True)
            rows = pl.ds(pl.multiple_of(hh * PEER_TOPK, PEER_TOPK), PEER_TOPK)
            it_ref[rows, lanes] = i_sel
            jt_ref[rows, lanes] = j_sel
            gt_ref[rows, lanes] = g
        return 0

    lax.fori_loop(0, PEER_HEADS, head, 0)
    isel_ref[...] = it_ref[...].T.astype(I32)
    jsel_ref[...] = jt_ref[...].T.astype(I32)
    g_ref[...] = gt_ref[...].T


def _peer_route(h2, wq, sk, tb):
    t, d = h2.shape
    tile = lambda i: (i, 0)
    one = pl.Buffered(1)
    return pl.pallas_call(
        functools.partial(_route_kernel, tb=tb),
        grid=(t // tb,),
        in_specs=[pl.BlockSpec((tb, d), tile),
                  pl.BlockSpec(wq.shape, lambda i: (0, 0, 0), pipeline_mode=one),
                  pl.BlockSpec(sk.shape, lambda i: (0, 0, 0), pipeline_mode=one)],
        out_specs=[pl.BlockSpec((tb, N_PAIRS), tile)] * 3,
        out_shape=[jax.ShapeDtypeStruct((t, N_PAIRS), I32),
                   jax.ShapeDtypeStruct((t, N_PAIRS), I32),
                   jax.ShapeDtypeStruct((t, N_PAIRS), F32)],
        scratch_shapes=[pltpu.VMEM((N_PAIRS, tb), F32)] * 3,
        compiler_params=pltpu.CompilerParams(
            dimension_semantics=("parallel",), vmem_limit_bytes=VMEM_LIMIT),
        name="peer_route",
    )(h2, wq, sk)


def _ffn_kernel(h2_ref, isel_ref, jsel_ref, g_ref, dnt_ref, up_ref, x1_ref, gate_ref, fg_ref,
                out_ref, w_ref, acc_ref, *, tb, ec):
    e = pl.program_id(1)

    @pl.when(e == 0)
    def _build():
        acc_ref[...] = jnp.zeros_like(acc_ref)
        key = lax.broadcasted_iota(I32, (N_KEYS, N_PAIRS), 0)

        def tok(t, _):
            ii = jnp.broadcast_to(isel_ref[pl.ds(t, 1), :], (N_KEYS, N_PAIRS))
            jj = jnp.broadcast_to(jsel_ref[pl.ds(t, 1), :], (N_KEYS, N_PAIRS))
            gg = jnp.broadcast_to(g_ref[pl.ds(t, 1), :], (N_KEYS, N_PAIRS))
            a = jnp.where(ii == key, 1.0, 0.0).astype(BF16)
            b = jnp.where(jj == key, gg, 0.0).astype(BF16)
            w_ref[pl.ds(pl.multiple_of(t * W_PITCH, 8), N_KEYS), :] = lax.dot_general(
                a, b, (((1,), (1,)), ((), ())), preferred_element_type=F32)
            return 0

        lax.fori_loop(0, tb, tok, 0, unroll=16)

    h2 = h2_ref[...]
    tile0 = e * (ec // LANES)
    for c in range(ec // MXU_DIM):
        cols = slice(c * MXU_DIM, (c + 1) * MXU_DIM)
        pre = jnp.dot(h2, dnt_ref[:, cols], preferred_element_type=F32)
        w = jnp.concatenate(
            [w_ref[pl.ds(tile0 + c * (MXU_DIM // LANES) + u, tb, stride=W_PITCH), :]
             for u in range(MXU_DIM // LANES)], axis=-1)
        z = (w * _gelu(pre)).astype(BF16)
        acc_ref[...] += jnp.dot(z, up_ref[cols, :], preferred_element_type=F32)

    @pl.when(e == pl.num_programs(1) - 1)
    def _finish():
        x2 = x1_ref[...] + gate_ref[0] * acc_ref[...]
        out_ref[...] = _rms(x2) * fg_ref[...]


def _peer_ffn(h2, isel, jsel, g, down_t, up, x1, gate_f, final_gain, tb, ec, tokens_per_batch):
    t, d = h2.shape
    n_exp = up.shape[0]
    tile = lambda i, e: (i, 0)
    tpb = tokens_per_batch // tb
    return pl.pallas_call(
        functools.partial(_ffn_kernel, tb=tb, ec=ec),
        grid=(t // tb, n_exp // ec),
        in_specs=[pl.BlockSpec((tb, d), tile),
                  pl.BlockSpec((tb, N_PAIRS), tile),
                  pl.BlockSpec((tb, N_PAIRS), tile),
                  pl.BlockSpec((tb, N_PAIRS), tile),
                  pl.BlockSpec((d, ec), lambda i, e: (0, e)),
                  pl.BlockSpec((ec, d), lambda i, e: (e, 0)),
                  pl.BlockSpec((tb, d), tile),
                  pl.BlockSpec((1, 1, d), lambda i, e: (i // tpb, 0, 0)),
                  pl.BlockSpec((1, d), lambda i, e: (0, 0))],
        out_specs=pl.BlockSpec((tb, d), tile),
        out_shape=jax.ShapeDtypeStruct((t, d), F32),
        scratch_shapes=[pltpu.VMEM((tb * W_PITCH, LANES), F32),
                        pltpu.VMEM((tb, d), F32)],
        compiler_params=pltpu.CompilerParams(
            dimension_semantics=("parallel", "arbitrary"), vmem_limit_bytes=VMEM_LIMIT),
        name="peer_ffn",
    )(h2, isel, jsel, g, down_t, up, x1, gate_f, final_gain)


def _layer(x, posf, invf, mod, l, p, tm, tq, tb_route, tb_ffn, ec):
    bsz, s, d = x.shape
    lam_init = 0.8 - 0.6 * math.exp(-0.3 * l)
    qt, k, vt, xp, gates = _mixer_in(
        x, posf, mod, p['norm_mix_gain'][l][None], invf, p['w_in'][l].astype(BF16),
        p['w_gate'][l].astype(BF16), p['b_gate'][l][None], tm)
    lam_vecs = jnp.stack([p['lambda_q1'][l], p['lambda_k1'][l],
                          p['lambda_q2'][l], p['lambda_k2'][l]])
    o = _diff_attn(lam_vecs, qt, k, vt, p['attn_subln_gain'][l][None], lam_init, tq, 4)
    x1, h2 = _merge_out(
        x, xp, o, gates, mod, p['norm_ffn_gain'][l][None], p['pool_w'][l].astype(BF16),
        p['pool_scale'][l][None], p['w_branch_pool'][l].astype(BF16),
        p['w_branch_attn'][l].astype(BF16), p['w_out'][l].astype(BF16), tm)
    h2f = h2.reshape(bsz * s, d)
    wq = p['peer_w_query'][l].astype(BF16).reshape(d, PEER_HEADS, 2 * PEER_HALF).transpose(1, 0, 2)
    isel, jsel, g = _peer_route(h2f, wq, p['peer_sub_keys'][l].astype(BF16), tb_route)
    return h2f, isel, jsel, g, x1.reshape(bsz * s, d)


def kernel(x, c, positions, norm_mix_gain, norm_ffn_gain, w_ada, b_ada, w_in, w_gate, b_gate,
           pool_w, pool_scale, lambda_q1, lambda_k1, lambda_q2, lambda_k2, attn_subln_gain,
           w_branch_pool, w_branch_attn, w_out, peer_w_query, peer_sub_keys, peer_down, peer_up,
           final_norm_gain):
    bsz, s, d = x.shape
    depth = w_in.shape[0]
    assert depth == 1, "final RMSNorm is fused into the last layer's peer_ffn call"
    p = dict(norm_mix_gain=norm_mix_gain, norm_ffn_gain=norm_ffn_gain, w_in=w_in, w_gate=w_gate,
             b_gate=b_gate, pool_w=pool_w, pool_scale=pool_scale, lambda_q1=lambda_q1,
             lambda_k1=lambda_k1, lambda_q2=lambda_q2, lambda_k2=lambda_k2,
             attn_subln_gain=attn_subln_gain, w_branch_pool=w_branch_pool,
             w_branch_attn=w_branch_attn, w_out=w_out, peer_w_query=peer_w_query,
             peer_sub_keys=peer_sub_keys)
    tm = min(256, s)
    tq = min(256, s)
    tb_route = 256
    tb_ffn = 256
    ec = 2048
    inv_freq = ROPE_THETA ** (-jnp.arange(0, ROT_DIM, 2, dtype=F32) / ROT_DIM)
    invf = jnp.tile(inv_freq, LANES // (ROT_DIM // 2))[None]
    posf = positions.astype(F32)[..., None]
    l = 0
    mod = _ada_mod(c, w_ada[l], b_ada[l]).reshape(bsz, N_MOD, d)
    h2f, isel, jsel, g, x1f = _layer(x, posf, invf, mod, l, p, tm, tq, tb_route, tb_ffn, ec)
    down_t = peer_down[l].astype(BF16).T
    out = _peer_ffn(h2f, isel, jsel, g, down_t, peer_up[l].astype(BF16), x1f,
                    mod[:, 5:6, :], final_norm_gain[None], tb_ffn, ec, s)
    return out.reshape(bsz, s, d)
```

```python
import functools
import math

import jax
import jax.numpy as jnp
from jax import lax
from jax.experimental import pallas as pl
from jax.experimental.pallas import tpu as pltpu

F32 = jnp.float32
BF16 = jnp.bfloat16
I32 = jnp.int32

EPS = 1e-6
NEG_INF = -1e30

N_ATTN_HEADS = 8
ATTN_HEAD_DIM = 64
ATTN_V_DIM = 128
ROT_DIM = 16
ROPE_THETA = 500000.0
POOL_WINDOWS = (2, 4, 8, 16)
POOL_GROUP_DIM = 256
POOL_HALO = 16
N_MOD = 6

PEER_HEADS = 8
N_KEYS = 128
PEER_HALF = 128
PEER_TOPK = 16
N_PAIRS = PEER_HEADS * PEER_TOPK

LANES = 128
MXU_DIM = 256
VMEM_LIMIT = 56 * 1024 * 1024

W_PITCH = 136
ROUTE_PITCH = 136


def _rms(x, eps=EPS):
    return x * lax.rsqrt(jnp.mean(x * x, axis=-1, keepdims=True) + eps)


def _gelu(x):
    return 0.5 * x * (1.0 + lax.erf(x * math.sqrt(0.5)))


def _ada_kernel(c_ref, w_ref, b_ref, o_ref):
    ca = jax.nn.silu(c_ref[...])
    o_ref[...] = jnp.dot(ca, w_ref[...], preferred_element_type=F32,
                         precision=lax.Precision.HIGHEST) + b_ref[...]


def _ada_mod(c, w, b):
    bsz, d = c.shape
    n = w.shape[1]
    tn = 1536
    return pl.pallas_call(
        _ada_kernel,
        grid=(n // tn,),
        in_specs=[pl.BlockSpec((bsz, d), lambda j: (0, 0)),
                  pl.BlockSpec((d, tn), lambda j: (0, j)),
                  pl.BlockSpec((1, tn), lambda j: (0, j))],
        out_specs=pl.BlockSpec((bsz, tn), lambda j: (0, j)),
        out_shape=jax.ShapeDtypeStruct((bsz, n), F32),
        compiler_params=pltpu.CompilerParams(vmem_limit_bytes=VMEM_LIMIT),
        name="ada_mod",
    )(c, w, b.reshape(1, n))


def _mixer_in_kernel(x_ref, pos_ref, mod_ref, gain_ref, invf_ref, win_ref, wg_ref, bg_ref,
                     qt_ref, k_ref, vt_ref, xp_ref, gt_ref):
    d = x_ref.shape[-1]
    x = x_ref[0]
    shift = mod_ref[0, 0:1, :]
    scale = mod_ref[0, 1:2, :]
    h = (_rms(x) * gain_ref[...]) * (1.0 + scale) + shift
    hb = h.astype(BF16)

    ang = pos_ref[0] * invf_ref[...]
    cos = jnp.cos(ang)
    sin = jnp.sin(ang)
    lane = lax.broadcasted_iota(I32, (1, LANES), 1) % ATTN_HEAD_DIM
    half = ROT_DIM // 2
    cm = jnp.where(lane < ROT_DIM, cos, 1.0)
    sa = jnp.where(lane < half, -sin, 0.0)
    sb = jnp.where((lane >= half) & (lane < ROT_DIM), sin, 0.0)

    def rope(t):
        return (t * cm + pltpu.roll(t, LANES - half, 1) * sa + pltpu.roll(t, half, 1) * sb)

    p = jnp.dot(hb, win_ref[:, 0:d], preferred_element_type=F32)
    for hh in range(d // LANES):
        sl = slice(hh * LANES, (hh + 1) * LANES)
        qt_ref[0, sl, :] = (rope(p[:, sl]) * ATTN_HEAD_DIM ** -0.5).T.astype(qt_ref.dtype)
    p = jnp.dot(hb, win_ref[:, d:2 * d], preferred_element_type=F32)
    for hh in range(d // LANES):
        sl = slice(hh * LANES, (hh + 1) * LANES)
        k_ref[0, :, sl] = rope(p[:, sl]).astype(k_ref.dtype)
    p = jnp.dot(hb, win_ref[:, 2 * d:3 * d], preferred_element_type=F32)
    for hh in range(d // LANES):
        sl = slice(hh * LANES, (hh + 1) * LANES)
        vt_ref[0, sl, :] = p[:, sl].T.astype(vt_ref.dtype)
    xp_ref[0] = jnp.dot(hb, win_ref[:, 3 * d:4 * d], preferred_element_type=F32)
    for c in range(2):
        g = jnp.dot(hb, wg_ref[:, c * d:(c + 1) * d], preferred_element_type=F32)
        gt_ref[0, :, c * d:(c + 1) * d] = jax.nn.sigmoid(
            g + bg_ref[:, c * d:(c + 1) * d]).astype(gt_ref.dtype)


def _mixer_in(x, posf, mod, gain, invf, w_in, w_gate, b_gate, tm):
    bsz, s, d = x.shape
    const = lambda b, i: (0, 0)
    tile = lambda b, i: (b, i, 0)
    ttile = lambda b, i: (b, 0, i)
    one = pl.Buffered(1)
    return pl.pallas_call(
        _mixer_in_kernel,
        grid=(bsz, s // tm),
        in_specs=[pl.BlockSpec((1, tm, d), tile),
                  pl.BlockSpec((1, tm, 1), tile),
                  pl.BlockSpec((1, N_MOD, d), lambda b, i: (b, 0, 0)),
                  pl.BlockSpec((1, d), const),
                  pl.BlockSpec((1, LANES), const),
                  pl.BlockSpec(w_in.shape, const, pipeline_mode=one),
                  pl.BlockSpec(w_gate.shape, const, pipeline_mode=one),
                  pl.BlockSpec((1, 2 * d), const)],
        out_specs=[pl.BlockSpec((1, d, tm), ttile), pl.BlockSpec((1, tm, d), tile),
                   pl.BlockSpec((1, d, tm), ttile), pl.BlockSpec((1, tm, d), tile),
                   pl.BlockSpec((1, tm, 2 * d), tile)],
        out_shape=[jax.ShapeDtypeStruct((bsz, d, s), BF16),
                   jax.ShapeDtypeStruct((bsz, s, d), BF16),
                   jax.ShapeDtypeStruct((bsz, d, s), BF16),
                   jax.ShapeDtypeStruct((bsz, s, d), F32),
                   jax.ShapeDtypeStruct((bsz, s, 2 * d), BF16)],
        compiler_params=pltpu.CompilerParams(
            dimension_semantics=("parallel", "parallel"), vmem_limit_bytes=VMEM_LIMIT),
        name="mixer_in",
    )(x, posf, mod, gain, invf, w_in, w_gate, b_gate)


def _attn_kernel(lam_ref, qt_ref, k_ref, vt_ref, gain_ref, o_ref, *, nblk, tq, hps, lam_init):
    sub = lax.broadcasted_iota(I32, (LANES, tq), 0)
    zero = jnp.zeros((LANES, tq), BF16)
    qq = []
    for h in range(hps):
        qt = qt_ref[0, h * LANES:(h + 1) * LANES, :]
        qq.append(jnp.concatenate([jnp.where(sub < ATTN_HEAD_DIM, qt, zero),
                                   jnp.where(sub >= ATTN_HEAD_DIM, qt, zero)], axis=1))
    m = [jnp.full((1, 2 * tq), NEG_INF, F32)] * hps
    l = [jnp.zeros((1, 2 * tq), F32)] * hps
    acc = [jnp.zeros((ATTN_V_DIM, 2 * tq), F32)] * hps

    def scores(j, h):
        return jnp.dot(k_ref[0, j * tq:(j + 1) * tq, h * LANES:(h + 1) * LANES], qq[h],
                       preferred_element_type=F32)

    s_next = [scores(0, h) for h in range(hps)]
    for j in range(nblk):
        for h in range(hps):
            s = s_next[h]
            if j + 1 < nblk:
                s_next[h] = scores(j + 1, h)
            if j == nblk - 1:
                kv = lax.broadcasted_iota(I32, s.shape, 0)
                qr = lax.broadcasted_iota(I32, s.shape, 1) % tq
                s = jnp.where(qr >= kv, s, NEG_INF)
            m_new = jnp.maximum(m[h], jnp.max(s, axis=0, keepdims=True))
            alpha = jnp.exp(m[h] - m_new)
            p = jnp.exp(s - m_new)
            l[h] = alpha * l[h] + jnp.sum(p, axis=0, keepdims=True)
            vt = vt_ref[0, h * LANES:(h + 1) * LANES, j * tq:(j + 1) * tq]
            acc[h] = alpha * acc[h] + jnp.dot(vt, p.astype(BF16),
                                              preferred_element_type=F32)
            m[h] = m_new

    lv = lam_ref[...]
    lam = (jnp.exp(jnp.sum(lv[0:1] * lv[1:2], axis=-1, keepdims=True))
           - jnp.exp(jnp.sum(lv[2:3] * lv[3:4], axis=-1, keepdims=True)) + lam_init)
    for h in range(hps):
        cols = slice(h * LANES, (h + 1) * LANES)
        ot = acc[h] / l[h]
        o = (ot[:, :tq] - lam * ot[:, tq:]).T
        o_ref[0, :, cols] = (_rms(o) * gain_ref[:, cols] * (1.0 - lam_init)).astype(o_ref.dtype)


def _diff_attn(lam_vecs, qt, k, vt, gain, lam_init, tq, hps):
    bsz, s, d = k.shape
    w = hps * LANES
    outs = []
    for qi in range(s // tq):
        kv_len = (qi + 1) * tq
        outs.append(pl.pallas_call(
            functools.partial(_attn_kernel, nblk=qi + 1, tq=tq, hps=hps, lam_init=lam_init),
            grid=(bsz, d // w),
            in_specs=[pl.BlockSpec(lam_vecs.shape, lambda b, h: (0, 0)),
                      pl.BlockSpec((1, w, tq), lambda b, h, qi=qi: (b, h, qi)),
                      pl.BlockSpec((1, kv_len, w), lambda b, h: (b, 0, h)),
                      pl.BlockSpec((1, w, kv_len), lambda b, h: (b, h, 0)),
                      pl.BlockSpec((1, w), lambda b, h: (0, h))],
            out_specs=pl.BlockSpec((1, tq, w), lambda b, h: (b, 0, h)),
            out_shape=jax.ShapeDtypeStruct((bsz, tq, d), BF16),
            compiler_params=pltpu.CompilerParams(
                dimension_semantics=("parallel", "parallel"), vmem_limit_bytes=VMEM_LIMIT),
            name=f"diff_attn_q{qi}",
        )(lam_vecs, qt, k, vt, gain))
    return jnp.concatenate(outs, axis=1)


def _merge_kernel(x_ref, xp_ref, halo_ref, o_ref, gt_ref, mod_ref, gain_ref, pw_ref, ps_ref,
                  wbp_ref, wba_ref, wout_ref, x1_ref, h2_ref, xs_ref, *, tm):
    i = pl.program_id(1)
    d = x_ref.shape[-1]
    halo = halo_ref[0]
    xs_ref[0:POOL_HALO, :] = jnp.where(i > 0, halo, jnp.zeros_like(halo))
    xs_ref[POOL_HALO:, :] = xp_ref[0]
    pos = i * tm + lax.broadcasted_iota(I32, (tm, 1), 0)
    y_pool = None
    ya = []
    for g, w in enumerate(POOL_WINDOWS):
        cols = slice(g * POOL_GROUP_DIM, (g + 1) * POOL_GROUP_DIM)
        tok = xs_ref[POOL_HALO:, cols]
        acc = tok
        for u in range(1, w):
            acc = acc + xs_ref[POOL_HALO - u:POOL_HALO - u + tm, cols]
        cnt = jnp.minimum(pos + 1, w).astype(F32)
        pooled = acc / cnt - tok
        yg = jnp.dot(pooled.astype(BF16), pw_ref[g], preferred_element_type=F32)
        ya.append((yg * ps_ref[:, cols]).astype(BF16))
    ya = jnp.concatenate(ya, axis=-1)
    y_pool = jnp.dot(ya, wbp_ref[...], preferred_element_type=F32)
    y_attn = jnp.dot(o_ref[0], wba_ref[...], preferred_element_type=F32)
    merged = (gt_ref[0, :, 0:d].astype(F32) * y_pool + gt_ref[0, :, d:2 * d].astype(F32) * y_attn)
    mix = jnp.dot(merged.astype(BF16), wout_ref[...], preferred_element_type=F32)
    x1 = x_ref[0] + mod_ref[0, 2:3, :] * mix
    x1_ref[0] = x1
    h2 = (_rms(x1) * gain_ref[...]) * (1.0 + mod_ref[0, 4:5, :]) + mod_ref[0, 3:4, :]
    h2_ref[0] = h2.astype(h2_ref.dtype)


def _merge_out(x, xp, o, gates, mod, gain_f, pool_w, pool_scale, wbp, wba, wout, tm):
    bsz, s, d = x.shape
    const2 = lambda b, i: (0, 0)
    tile = lambda b, i: (b, i, 0)
    hb = tm // POOL_HALO
    one = pl.Buffered(1)
    return pl.pallas_call(
        functools.partial(_merge_kernel, tm=tm),
        grid=(bsz, s // tm),
        in_specs=[pl.BlockSpec((1, tm, d), tile),
                  pl.BlockSpec((1, tm, d), tile),
                  pl.BlockSpec((1, POOL_HALO, d), lambda b, i: (b, jnp.maximum(i * hb - 1, 0), 0)),
                  pl.BlockSpec((1, tm, d), tile),
                  pl.BlockSpec((1, tm, 2 * d), tile),
                  pl.BlockSpec((1, N_MOD, d), lambda b, i: (b, 0, 0)),
                  pl.BlockSpec((1, d), const2),
                  pl.BlockSpec(pool_w.shape, lambda b, i: (0, 0, 0), pipeline_mode=one),
                  pl.BlockSpec((1, d), const2),
                  pl.BlockSpec(wbp.shape, const2, pipeline_mode=one),
                  pl.BlockSpec(wba.shape, const2, pipeline_mode=one),
                  pl.BlockSpec(wout.shape, const2, pipeline_mode=one)],
        out_specs=[pl.BlockSpec((1, tm, d), tile), pl.BlockSpec((1, tm, d), tile)],
        out_shape=[jax.ShapeDtypeStruct((bsz, s, d), F32),
                   jax.ShapeDtypeStruct((bsz, s, d), BF16)],
        scratch_shapes=[pltpu.VMEM((tm + POOL_HALO, d), F32)],
        compiler_params=pltpu.CompilerParams(
            dimension_semantics=("parallel", "parallel"), vmem_limit_bytes=VMEM_LIMIT),
        name="merge_out",
    )(x, xp, xp, o, gates, mod, gain_f, pool_w, pool_scale, wbp, wba, wout)


def _staircase():
    return [(a, b) for a in range(PEER_TOPK) for b in range(PEER_TOPK)
            if (a + 1) * (b + 1) <= PEER_TOPK]


def _tree(op, xs):
    xs = list(xs)
    while len(xs) > 1:
        xs = [op(xs[i], xs[i + 1]) if i + 1 < len(xs) else xs[i] for i in range(0, len(xs), 2)]
    return xs[0]


def _route_kernel(h2_ref, wq_ref, sk_ref, isel_ref, jsel_ref, g_ref,
                  slab_ref, km_ref, sv_ref, si_ref, oi_ref, oj_ref, og_ref, *, tb):
    n_slab = tb // LANES
    big = 1e9
    h2 = h2_ref[...]
    cands = _staircase()

    def head(hh, _):
        q = jnp.dot(h2, wq_ref[hh], preferred_element_type=F32).astype(BF16)
        for half in range(2):
            qh = q[:, half * PEER_HALF:(half + 1) * PEER_HALF]
            st = lax.dot_general(sk_ref[half], qh, (((1,), (1,)), ((), ())),
                                 preferred_element_type=F32)
            for c in range(n_slab):
                slab_ref[c * ROUTE_PITCH:c * ROUTE_PITCH + N_KEYS, :] = (
                    st[:, c * LANES:(c + 1) * LANES])
            for k in range(N_KEYS):
                km_ref[k] = slab_ref[pl.ds(k, n_slab, stride=ROUTE_PITCH), :]

            def extract(r, pos_prev):
                part = [None] * 8
                for k in range(N_KEYS):
                    v = jnp.where(pos_prev == float(k), -jnp.inf, km_ref[k])
                    km_ref[k] = v
                    part[k % 8] = v if part[k % 8] is None else jnp.maximum(part[k % 8], v)
                m = _tree(jnp.maximum, part)
                part = [None] * 8
                for k in range(N_KEYS):
                    c = jnp.where(km_ref[k] == m, float(k), big)
                    part[k % 8] = c if part[k % 8] is None else jnp.minimum(part[k % 8], c)
                pos = _tree(jnp.minimum, part)
                sv_ref[half, r] = m
                si_ref[half, r] = pos
                return pos

            lax.fori_loop(0, PEER_TOPK, extract, jnp.full((n_slab, LANES), -1.0, F32))

        sv0 = [sv_ref[0, a] for a in range(PEER_TOPK)]
        sv1 = [sv_ref[1, b] for b in range(PEER_TOPK)]
        cand = [sv0[a] + sv1[b] for a, b in cands]
        code = [float(a * PEER_TOPK + b) for a, b in cands]
        top_s, top_pos = [], []
        for _ in range(PEER_TOPK):
            m = _tree(jnp.maximum, cand)
            pos = _tree(jnp.minimum, [jnp.where(c == m, cd, big) for c, cd in zip(cand, code)])
            top_s.append(m)
            top_pos.append(pos)
            cand = [jnp.where(pos == cd, -jnp.inf, c) for c, cd in zip(cand, code)]
        si0 = [si_ref[0, a] for a in range(PEER_TOPK)]
        si1 = [si_ref[1, b] for b in range(PEER_TOPK)]
        e = [jnp.exp(t - top_s[0]) for t in top_s]
        denom = _tree(jnp.add, e)
        for r in range(PEER_TOPK):
            a_sel = jnp.floor(top_pos[r] * (1.0 / PEER_TOPK))
            b_sel = top_pos[r] - a_sel * PEER_TOPK
            i_sel = si0[0]
            j_sel = si1[0]
            for a in range(1, PEER_TOPK):
                i_sel = jnp.where(a_sel == float(a), si0[a], i_sel)
                j_sel = jnp.where(b_sel == float(a), si1[a], j_sel)
            row = hh * PEER_TOPK + r
            oi_ref[row] = i_sel
            oj_ref[row] = j_sel
            og_ref[row] = e[r] / denom
        return 0

    lax.fori_loop(0, PEER_HEADS, head, 0)

    for src, dst in ((oi_ref, isel_ref), (oj_ref, jsel_ref), (og_ref, g_ref)):
        for p in range(N_PAIRS):
            slab_ref[pl.ds(p, n_slab, stride=ROUTE_PITCH), :] = src[p]
        for c in range(n_slab):
            blk = slab_ref[c * ROUTE_PITCH:c * ROUTE_PITCH + N_PAIRS, :].T
            dst[c * LANES:(c + 1) * LANES, :] = blk.astype(dst.dtype)


def _peer_route(h2, wq, sk, tb):
    t, d = h2.shape
    tile = lambda i: (i, 0)
    one = pl.Buffered(1)
    n_slab = tb // LANES
    vregs = lambda n: pltpu.VMEM((n, n_slab, LANES), F32)
    return pl.pallas_call(
        functools.partial(_route_kernel, tb=tb),
        grid=(t // tb,),
        in_specs=[pl.BlockSpec((tb, d), tile),
                  pl.BlockSpec(wq.shape, lambda i: (0, 0, 0), pipeline_mode=one),
                  pl.BlockSpec(sk.shape, lambda i: (0, 0, 0), pipeline_mode=one)],
        out_specs=[pl.BlockSpec((tb, N_PAIRS), tile)] * 3,
        out_shape=[jax.ShapeDtypeStruct((t, N_PAIRS), I32),
                   jax.ShapeDtypeStruct((t, N_PAIRS), I32),
                   jax.ShapeDtypeStruct((t, N_PAIRS), F32)],
        scratch_shapes=[pltpu.VMEM((n_slab * ROUTE_PITCH, LANES), F32),
                        vregs(N_KEYS),
                        pltpu.VMEM((2, PEER_TOPK, n_slab, LANES), F32),
                        pltpu.VMEM((2, PEER_TOPK, n_slab, LANES), F32),
                        vregs(N_PAIRS), vregs(N_PAIRS), vregs(N_PAIRS)],
        compiler_params=pltpu.CompilerParams(
            dimension_semantics=("parallel",), vmem_limit_bytes=VMEM_LIMIT),
        name="peer_route",
    )(h2, wq, sk)


def _ffn_kernel(h2_ref, isel_ref, jsel_ref, g_ref, dnt_ref, up_ref, x1_ref, gate_ref, fg_ref,
                out_ref, w_ref, acc_ref, *, tb, ec):
    e = pl.program_id(1)

    @pl.when(e == 0)
    def _build():
        acc_ref[...] = jnp.zeros_like(acc_ref)
        key = lax.broadcasted_iota(I32, (N_KEYS, N_PAIRS), 0)

        def tok(t, _):
            ii = jnp.broadcast_to(isel_ref[pl.ds(t, 1), :], (N_KEYS, N_PAIRS))
            jj = jnp.broadcast_to(jsel_ref[pl.ds(t, 1), :], (N_KEYS, N_PAIRS))
            gg = jnp.broadcast_to(g_ref[pl.ds(t, 1), :], (N_KEYS, N_PAIRS))
            a = jnp.where(ii == key, 1.0, 0.0).astype(BF16)
            b = jnp.where(jj == key, gg, 0.0).astype(BF16)
            w_ref[pl.ds(pl.multiple_of(t * W_PITCH, 8), N_KEYS), :] = lax.dot_general(
                a, b, (((1,), (1,)), ((), ())), preferred_element_type=F32)
            return 0

        lax.fori_loop(0, tb, tok, 0, unroll=16)

    h2 = h2_ref[...]
    tile0 = e * (ec // LANES)
    for c in range(ec // MXU_DIM):
        cols = slice(c * MXU_DIM, (c + 1) * MXU_DIM)
        pre = jnp.dot(h2, dnt_ref[:, cols], preferred_element_type=F32)
        w = jnp.concatenate(
            [w_ref[pl.ds(tile0 + c * (MXU_DIM // LANES) + u, tb, stride=W_PITCH), :]
             for u in range(MXU_DIM // LANES)], axis=-1)
        z = (w * _gelu(pre)).astype(BF16)
        acc_ref[...] += jnp.dot(z, up_ref[cols, :], preferred_element_type=F32)

    @pl.when(e == pl.num_programs(1) - 1)
    def _finish():
        x2 = x1_ref[...] + gate_ref[0] * acc_ref[...]
        out_ref[...] = _rms(x2) * fg_ref[...]


def _peer_ffn(h2, isel, jsel, g, down_t, up, x1, gate_f, final_gain, tb, ec, tokens_per_batch):
    t, d = h2.shape
    n_exp = up.shape[0]
    tile = lambda i, e: (i, 0)
    tpb = tokens_per_batch // tb
    return pl.pallas_call(
        functools.partial(_ffn_kernel, tb=tb, ec=ec),
        grid=(t // tb, n_exp // ec),
        in_specs=[pl.BlockSpec((tb, d), tile),
                  pl.BlockSpec((tb, N_PAIRS), tile),
                  pl.BlockSpec((tb, N_PAIRS), tile),
                  pl.BlockSpec((tb, N_PAIRS), tile),
                  pl.BlockSpec((d, ec), lambda i, e: (0, e)),
                  pl.BlockSpec((ec, d), lambda i, e: (e, 0)),
                  pl.BlockSpec((tb, d), tile),
                  pl.BlockSpec((1, 1, d), lambda i, e: (i // tpb, 0, 0)),
                  pl.BlockSpec((1, d), lambda i, e: (0, 0))],
        out_specs=pl.BlockSpec((tb, d), tile),
        out_shape=jax.ShapeDtypeStruct((t, d), F32),
        scratch_shapes=[pltpu.VMEM((tb * W_PITCH, LANES), F32),
                        pltpu.VMEM((tb, d), F32)],
        compiler_params=pltpu.CompilerParams(
            dimension_semantics=("parallel", "arbitrary"), vmem_limit_bytes=VMEM_LIMIT),
        name="peer_ffn",
    )(h2, isel, jsel, g, down_t, up, x1, gate_f, final_gain)


def _layer(x, posf, invf, mod, l, p, tm, tq, tb_route, tb_ffn, ec):
    bsz, s, d = x.shape
    lam_init = 0.8 - 0.6 * math.exp(-0.3 * l)
    qt, k, vt, xp, gates = _mixer_in(
        x, posf, mod, p['norm_mix_gain'][l][None], invf, p['w_in'][l].astype(BF16),
        p['w_gate'][l].astype(BF16), p['b_gate'][l][None], tm)
    lam_vecs = jnp.stack([p['lambda_q1'][l], p['lambda_k1'][l],
                          p['lambda_q2'][l], p['lambda_k2'][l]])
    o = _diff_attn(lam_vecs, qt, k, vt, p['attn_subln_gain'][l][None], lam_init, tq, 4)
    x1, h2 = _merge_out(
        x, xp, o, gates, mod, p['norm_ffn_gain'][l][None], p['pool_w'][l].astype(BF16),
        p['pool_scale'][l][None], p['w_branch_pool'][l].astype(BF16),
        p['w_branch_attn'][l].astype(BF16), p['w_out'][l].astype(BF16), tm)
    h2f = h2.reshape(bsz * s, d)
    wq = p['peer_w_query'][l].astype(BF16).reshape(d, PEER_HEADS, 2 * PEER_HALF).transpose(1, 0, 2)
    isel, jsel, g = _peer_route(h2f, wq, p['peer_sub_keys'][l].astype(BF16), tb_route)
    return h2f, isel, jsel, g, x1.reshape(bsz * s, d)


def kernel(x, c, positions, norm_mix_gain, norm_ffn_gain, w_ada, b_ada, w_in, w_gate, b_gate,
           pool_w, pool_scale, lambda_q1, lambda_k1, lambda_q2, lambda_k2, attn_subln_gain,
           w_branch_pool, w_branch_attn, w_out, peer_w_query, peer_sub_keys, peer_down, peer_up,
           final_norm_gain):
    bsz, s, d = x.shape
    depth = w_in.shape[0]
    assert depth == 1, "final RMSNorm is fused into the last layer's peer_ffn call"
    p = dict(norm_mix_gain=norm_mix_gain, norm_ffn_gain=norm_ffn_gain, w_in=w_in, w_gate=w_gate,
             b_gate=b_gate, pool_w=pool_w, pool_scale=pool_scale, lambda_q1=lambda_q1,
             lambda_k1=lambda_k1, lambda_q2=lambda_q2, lambda_k2=lambda_k2,
             attn_subln_gain=attn_subln_gain, w_branch_pool=w_branch_pool,
             w_branch_attn=w_branch_attn, w_out=w_out, peer_w_query=peer_w_query,
             peer_sub_keys=peer_sub_keys)
    tm = min(256, s)
    tq = min(256, s)
    tb_route = 1024
    tb_ffn = 256
    ec = 2048
    inv_freq = ROPE_THETA ** (-jnp.arange(0, ROT_DIM, 2, dtype=F32) / ROT_DIM)
    invf = jnp.tile(inv_freq, LANES // (ROT_DIM // 2))[None]
    posf = positions.astype(F32)[..., None]
    l = 0
    mod = _ada_mod(c, w_ada[l], b_ada[l]).reshape(bsz, N_MOD, d)
    h2f, isel, jsel, g, x1f = _layer(x, posf, invf, mod, l, p, tm, tq, tb_route, tb_ffn, ec)
    down_t = peer_down[l].astype(BF16).T
    out = _peer_ffn(h2f, isel, jsel, g, down_t, peer_up[l].astype(BF16), x1f,
                    mod[:, 5:6, :], final_norm_gain[None], tb_ffn, ec, s)
    return out.reshape(bsz, s, d)
```

```python
import functools
import math

import jax
import jax.numpy as jnp
from jax import lax
from jax.experimental import pallas as pl
from jax.experimental.pallas import tpu as pltpu

F32 = jnp.float32
BF16 = jnp.bfloat16
I32 = jnp.int32

EPS = 1e-6
NEG_INF = -1e30

N_ATTN_HEADS = 8
ATTN_HEAD_DIM = 64
ATTN_V_DIM = 128
ROT_DIM = 16
ROPE_THETA = 500000.0
POOL_WINDOWS = (2, 4, 8, 16)
POOL_GROUP_DIM = 256
POOL_HALO = 16
N_MOD = 6

PEER_HEADS = 8
N_KEYS = 128
PEER_HALF = 128
PEER_TOPK = 16
N_PAIRS = PEER_HEADS * PEER_TOPK

LANES = 128
MXU_DIM = 256
VMEM_LIMIT = 56 * 1024 * 1024

W_PITCH = 136
ROUTE_PITCH = 136


def _rms(x, eps=EPS):
    return x * lax.rsqrt(jnp.mean(x * x, axis=-1, keepdims=True) + eps)


def _gelu(x):
    return 0.5 * x * (1.0 + lax.erf(x * math.sqrt(0.5)))


def _ada_kernel(c_ref, w_ref, b_ref, o_ref):
    ca = jax.nn.silu(c_ref[...])
    o_ref[...] = jnp.dot(ca, w_ref[...], preferred_element_type=F32,
                         precision=lax.Precision.HIGHEST) + b_ref[...]


def _ada_mod(c, w, b):
    bsz, d = c.shape
    n = w.shape[1]
    tn = 1536
    return pl.pallas_call(
        _ada_kernel,
        grid=(n // tn,),
        in_specs=[pl.BlockSpec((bsz, d), lambda j: (0, 0)),
                  pl.BlockSpec((d, tn), lambda j: (0, j)),
                  pl.BlockSpec((1, tn), lambda j: (0, j))],
        out_specs=pl.BlockSpec((bsz, tn), lambda j: (0, j)),
        out_shape=jax.ShapeDtypeStruct((bsz, n), F32),
        compiler_params=pltpu.CompilerParams(vmem_limit_bytes=VMEM_LIMIT),
        name="ada_mod",
    )(c, w, b.reshape(1, n))


def _mixer_in_kernel(x_ref, pos_ref, mod_ref, gain_ref, invf_ref, win_ref, wg_ref, bg_ref,
                     qt_ref, k_ref, vt_ref, xp_ref, gt_ref):
    d = x_ref.shape[-1]
    x = x_ref[0]
    shift = mod_ref[0, 0:1, :]
    scale = mod_ref[0, 1:2, :]
    h = (_rms(x) * gain_ref[...]) * (1.0 + scale) + shift
    hb = h.astype(BF16)

    ang = pos_ref[0] * invf_ref[...]
    cos = jnp.cos(ang)
    sin = jnp.sin(ang)
    lane = lax.broadcasted_iota(I32, (1, LANES), 1) % ATTN_HEAD_DIM
    half = ROT_DIM // 2
    cm = jnp.where(lane < ROT_DIM, cos, 1.0)
    sa = jnp.where(lane < half, -sin, 0.0)
    sb = jnp.where((lane >= half) & (lane < ROT_DIM), sin, 0.0)

    def rope(t):
        return (t * cm + pltpu.roll(t, LANES - half, 1) * sa + pltpu.roll(t, half, 1) * sb)

    p = jnp.dot(hb, win_ref[:, 0:d], preferred_element_type=F32)
    for hh in range(d // LANES):
        sl = slice(hh * LANES, (hh + 1) * LANES)
        qt_ref[0, sl, :] = (rope(p[:, sl]) * ATTN_HEAD_DIM ** -0.5).T.astype(qt_ref.dtype)
    p = jnp.dot(hb, win_ref[:, d:2 * d], preferred_element_type=F32)
    for hh in range(d // LANES):
        sl = slice(hh * LANES, (hh + 1) * LANES)
        k_ref[0, :, sl] = rope(p[:, sl]).astype(k_ref.dtype)
    p = jnp.dot(hb, win_ref[:, 2 * d:3 * d], preferred_element_type=F32)
    for hh in range(d // LANES):
        sl = slice(hh * LANES, (hh + 1) * LANES)
        vt_ref[0, sl, :] = p[:, sl].T.astype(vt_ref.dtype)
    xp_ref[0] = jnp.dot(hb, win_ref[:, 3 * d:4 * d], preferred_element_type=F32)
    for c in range(2):
        g = jnp.dot(hb, wg_ref[:, c * d:(c + 1) * d], preferred_element_type=F32)
        gt_ref[0, :, c * d:(c + 1) * d] = jax.nn.sigmoid(
            g + bg_ref[:, c * d:(c + 1) * d]).astype(gt_ref.dtype)


def _mixer_in(x, posf, mod, gain, invf, w_in, w_gate, b_gate, tm):
    bsz, s, d = x.shape
    const = lambda b, i: (0, 0)
    tile = lambda b, i: (b, i, 0)
    ttile = lambda b, i: (b, 0, i)
    one = pl.Buffered(1)
    return pl.pallas_call(
        _mixer_in_kernel,
        grid=(bsz, s // tm),
        in_specs=[pl.BlockSpec((1, tm, d), tile),
                  pl.BlockSpec((1, tm, 1), tile),
                  pl.BlockSpec((1, N_MOD, d), lambda b, i: (b, 0, 0)),
                  pl.BlockSpec((1, d), const),
                  pl.BlockSpec((1, LANES), const),
                  pl.BlockSpec(w_in.shape, const, pipeline_mode=one),
                  pl.BlockSpec(w_gate.shape, const, pipeline_mode=one),
                  pl.BlockSpec((1, 2 * d), const)],
        out_specs=[pl.BlockSpec((1, d, tm), ttile), pl.BlockSpec((1, tm, d), tile),
                   pl.BlockSpec((1, d, tm), ttile), pl.BlockSpec((1, tm, d), tile),
                   pl.BlockSpec((1, tm, 2 * d), tile)],
        out_shape=[jax.ShapeDtypeStruct((bsz, d, s), BF16),
                   jax.ShapeDtypeStruct((bsz, s, d), BF16),
                   jax.ShapeDtypeStruct((bsz, d, s), BF16),
                   jax.ShapeDtypeStruct((bsz, s, d), F32),
                   jax.ShapeDtypeStruct((bsz, s, 2 * d), BF16)],
        compiler_params=pltpu.CompilerParams(
            dimension_semantics=("parallel", "parallel"), vmem_limit_bytes=VMEM_LIMIT),
        name="mixer_in",
    )(x, posf, mod, gain, invf, w_in, w_gate, b_gate)


def _attn_kernel(lam_ref, qt_ref, k_ref, vt_ref, gain_ref, o_ref, *, nblk, tq, hps, lam_init):
    sub = lax.broadcasted_iota(I32, (LANES, tq), 0)
    zero = jnp.zeros((LANES, tq), BF16)
    qq = []
    for h in range(hps):
        qt = qt_ref[0, h * LANES:(h + 1) * LANES, :]
        qq.append(jnp.concatenate([jnp.where(sub < ATTN_HEAD_DIM, qt, zero),
                                   jnp.where(sub >= ATTN_HEAD_DIM, qt, zero)], axis=1))
    m = [jnp.full((1, 2 * tq), NEG_INF, F32)] * hps
    l = [jnp.zeros((1, 2 * tq), F32)] * hps
    acc = [jnp.zeros((ATTN_V_DIM, 2 * tq), F32)] * hps

    def scores(j, h):
        return jnp.dot(k_ref[0, j * tq:(j + 1) * tq, h * LANES:(h + 1) * LANES], qq[h],
                       preferred_element_type=F32)

    s_next = [scores(0, h) for h in range(hps)]
    for j in range(nblk):
        for h in range(hps):
            s = s_next[h]
            if j + 1 < nblk:
                s_next[h] = scores(j + 1, h)
            if j == nblk - 1:
                kv = lax.broadcasted_iota(I32, s.shape, 0)
                qr = lax.broadcasted_iota(I32, s.shape, 1) % tq
                s = jnp.where(qr >= kv, s, NEG_INF)
            m_new = jnp.maximum(m[h], jnp.max(s, axis=0, keepdims=True))
            alpha = jnp.exp(m[h] - m_new)
            p = jnp.exp(s - m_new)
            l[h] = alpha * l[h] + jnp.sum(p, axis=0, keepdims=True)
            vt = vt_ref[0, h * LANES:(h + 1) * LANES, j * tq:(j + 1) * tq]
            acc[h] = alpha * acc[h] + jnp.dot(vt, p.astype(BF16),
                                              preferred_element_type=F32)
            m[h] = m_new

    lv = lam_ref[...]
    lam = (jnp.exp(jnp.sum(lv[0:1] * lv[1:2], axis=-1, keepdims=True))
           - jnp.exp(jnp.sum(lv[2:3] * lv[3:4], axis=-1, keepdims=True)) + lam_init)
    for h in range(hps):
        cols = slice(h * LANES, (h + 1) * LANES)
        ot = acc[h] * (1.0 / l[h])
        o = (ot[:, :tq] - lam * ot[:, tq:]).T
        o_ref[0, :, cols] = (_rms(o) * gain_ref[:, cols] * (1.0 - lam_init)).astype(o_ref.dtype)


def _diff_attn(lam_vecs, qt, k, vt, gain, lam_init, tq, hps):
    bsz, s, d = k.shape
    w = hps * LANES
    outs = []
    for qi in range(s // tq):
        kv_len = (qi + 1) * tq
        outs.append(pl.pallas_call(
            functools.partial(_attn_kernel, nblk=qi + 1, tq=tq, hps=hps, lam_init=lam_init),
            grid=(bsz, d // w),
            in_specs=[pl.BlockSpec(lam_vecs.shape, lambda b, h: (0, 0)),
                      pl.BlockSpec((1, w, tq), lambda b, h, qi=qi: (b, h, qi)),
                      pl.BlockSpec((1, kv_len, w), lambda b, h: (b, 0, h)),
                      pl.BlockSpec((1, w, kv_len), lambda b, h: (b, h, 0)),
                      pl.BlockSpec((1, w), lambda b, h: (0, h))],
            out_specs=pl.BlockSpec((1, tq, w), lambda b, h: (b, 0, h)),
            out_shape=jax.ShapeDtypeStruct((bsz, tq, d), BF16),
            compiler_params=pltpu.CompilerParams(
                dimension_semantics=("parallel", "parallel"), vmem_limit_bytes=VMEM_LIMIT),
            name=f"diff_attn_q{qi}",
        )(lam_vecs, qt, k, vt, gain))
    return jnp.concatenate(outs, axis=1)


def _merge_kernel(x_ref, xp_ref, halo_ref, o_ref, gt_ref, mod_ref, gain_ref, pw_ref, ps_ref,
                  wbp_ref, wba_ref, wout_ref, x1_ref, h2_ref, xs_ref, *, tm):
    i = pl.program_id(1)
    d = x_ref.shape[-1]
    halo = halo_ref[0]
    xs_ref[0:POOL_HALO, :] = jnp.where(i > 0, halo, jnp.zeros_like(halo))
    xs_ref[POOL_HALO:, :] = xp_ref[0]
    pos = i * tm + lax.broadcasted_iota(I32, (tm, 1), 0)
    y_pool = None
    ya = []
    for g, w in enumerate(POOL_WINDOWS):
        cols = slice(g * POOL_GROUP_DIM, (g + 1) * POOL_GROUP_DIM)
        tok = xs_ref[POOL_HALO:, cols]
        acc = tok
        for u in range(1, w):
            acc = acc + xs_ref[POOL_HALO - u:POOL_HALO - u + tm, cols]
        cnt = jnp.minimum(pos + 1, w).astype(F32)
        pooled = acc / cnt - tok
        yg = jnp.dot(pooled.astype(BF16), pw_ref[g], preferred_element_type=F32)
        ya.append((yg * ps_ref[:, cols]).astype(BF16))
    ya = jnp.concatenate(ya, axis=-1)
    y_pool = jnp.dot(ya, wbp_ref[...], preferred_element_type=F32)
    y_attn = jnp.dot(o_ref[0], wba_ref[...], preferred_element_type=F32)
    merged = (gt_ref[0, :, 0:d].astype(F32) * y_pool + gt_ref[0, :, d:2 * d].astype(F32) * y_attn)
    mix = jnp.dot(merged.astype(BF16), wout_ref[...], preferred_element_type=F32)
    x1 = x_ref[0] + mod_ref[0, 2:3, :] * mix
    x1_ref[0] = x1
    h2 = (_rms(x1) * gain_ref[...]) * (1.0 + mod_ref[0, 4:5, :]) + mod_ref[0, 3:4, :]
    h2_ref[0] = h2.astype(h2_ref.dtype)


def _merge_out(x, xp, o, gates, mod, gain_f, pool_w, pool_scale, wbp, wba, wout, tm):
    bsz, s, d = x.shape
    const2 = lambda b, i: (0, 0)
    tile = lambda b, i: (b, i, 0)
    hb = tm // POOL_HALO
    one = pl.Buffered(1)
    return pl.pallas_call(
        functools.partial(_merge_kernel, tm=tm),
        grid=(bsz, s // tm),
        in_specs=[pl.BlockSpec((1, tm, d), tile),
                  pl.BlockSpec((1, tm, d), tile),
                  pl.BlockSpec((1, POOL_HALO, d), lambda b, i: (b, jnp.maximum(i * hb - 1, 0), 0)),
                  pl.BlockSpec((1, tm, d), tile),
                  pl.BlockSpec((1, tm, 2 * d), tile),
                  pl.BlockSpec((1, N_MOD, d), lambda b, i: (b, 0, 0)),
                  pl.BlockSpec((1, d), const2),
                  pl.BlockSpec(pool_w.shape, lambda b, i: (0, 0, 0), pipeline_mode=one),
                  pl.BlockSpec((1, d), const2),
                  pl.BlockSpec(wbp.shape, const2, pipeline_mode=one),
                  pl.BlockSpec(wba.shape, const2, pipeline_mode=one),
                  pl.BlockSpec(wout.shape, const2, pipeline_mode=one)],
        out_specs=[pl.BlockSpec((1, tm, d), tile), pl.BlockSpec((1, tm, d), tile)],
        out_shape=[jax.ShapeDtypeStruct((bsz, s, d), F32),
                   jax.ShapeDtypeStruct((bsz, s, d), BF16)],
        scratch_shapes=[pltpu.VMEM((tm + POOL_HALO, d), F32)],
        compiler_params=pltpu.CompilerParams(
            dimension_semantics=("parallel", "parallel"), vmem_limit_bytes=VMEM_LIMIT),
        name="merge_out",
    )(x, xp, xp, o, gates, mod, gain_f, pool_w, pool_scale, wbp, wba, wout)


def _staircase():
    return [(a, b) for a in range(PEER_TOPK) for b in range(PEER_TOPK)
            if (a + 1) * (b + 1) <= PEER_TOPK]


def _tree(op, xs):
    xs = list(xs)
    while len(xs) > 1:
        xs = [op(xs[i], xs[i + 1]) if i + 1 < len(xs) else xs[i] for i in range(0, len(xs), 2)]
    return xs[0]


def _route_kernel(h2_ref, wq_ref, sk_ref, isel_ref, jsel_ref, g_ref,
                  slab_ref, km_ref, sv_ref, si_ref, oi_ref, oj_ref, og_ref, *, tb):
    n_slab = tb // LANES
    big = 1e9
    h2 = h2_ref[...]
    cands = _staircase()

    def head(hh, _):
        q = jnp.dot(h2, wq_ref[hh], preferred_element_type=F32).astype(BF16)
        for half in range(2):
            qh = q[:, half * PEER_HALF:(half + 1) * PEER_HALF]
            st = lax.dot_general(sk_ref[half], qh, (((1,), (1,)), ((), ())),
                                 preferred_element_type=F32)
            for c in range(n_slab):
                slab_ref[c * ROUTE_PITCH:c * ROUTE_PITCH + N_KEYS, :] = (
                    st[:, c * LANES:(c + 1) * LANES])
            for k in range(N_KEYS):
                km_ref[k] = slab_ref[pl.ds(k, n_slab, stride=ROUTE_PITCH), :]

            def extract(r, pos_prev):
                part = [None] * 8
                for k in range(N_KEYS):
                    v = jnp.where(pos_prev == float(k), -jnp.inf, km_ref[k])
                    km_ref[k] = v
                    part[k % 8] = v if part[k % 8] is None else jnp.maximum(part[k % 8], v)
                m = _tree(jnp.maximum, part)
                part = [None] * 8
                for k in range(N_KEYS):
                    c = jnp.where(km_ref[k] == m, float(k), big)
                    part[k % 8] = c if part[k % 8] is None else jnp.minimum(part[k % 8], c)
                pos = _tree(jnp.minimum, part)
                sv_ref[half, r] = m
                si_ref[half, r] = pos
                return pos

            lax.fori_loop(0, PEER_TOPK, extract, jnp.full((n_slab, LANES), -1.0, F32))

        sv0 = [sv_ref[0, a] for a in range(PEER_TOPK)]
        sv1 = [sv_ref[1, b] for b in range(PEER_TOPK)]
        cand = [sv0[a] + sv1[b] for a, b in cands]
        code = [float(a * PEER_TOPK + b) for a, b in cands]
        top_s, top_pos = [], []
        for _ in range(PEER_TOPK):
            m = _tree(jnp.maximum, cand)
            pos = _tree(jnp.minimum, [jnp.where(c == m, cd, big) for c, cd in zip(cand, code)])
            top_s.append(m)
            top_pos.append(pos)
            cand = [jnp.where(pos == cd, -jnp.inf, c) for c, cd in zip(cand, code)]
        si0 = [si_ref[0, a] for a in range(PEER_TOPK)]
        si1 = [si_ref[1, b] for b in range(PEER_TOPK)]
        e = [jnp.exp(t - top_s[0]) for t in top_s]
        denom = _tree(jnp.add, e)
        for r in range(PEER_TOPK):
            a_sel = jnp.floor(top_pos[r] * (1.0 / PEER_TOPK))
            b_sel = top_pos[r] - a_sel * PEER_TOPK
            i_sel = si0[0]
            j_sel = si1[0]
            for a in range(1, PEER_TOPK):
                i_sel = jnp.where(a_sel == float(a), si0[a], i_sel)
                j_sel = jnp.where(b_sel == float(a), si1[a], j_sel)
            row = hh * PEER_TOPK + r
            oi_ref[row] = i_sel
            oj_ref[row] = j_sel
            og_ref[row] = e[r] / denom
        return 0

    lax.fori_loop(0, PEER_HEADS, head, 0)

    for src, dst in ((oi_ref, isel_ref), (oj_ref, jsel_ref), (og_ref, g_ref)):
        for p in range(N_PAIRS):
            slab_ref[pl.ds(p, n_slab, stride=ROUTE_PITCH), :] = src[p]
        for c in range(n_slab):
            blk = slab_ref[c * ROUTE_PITCH:c * ROUTE_PITCH + N_PAIRS, :].T
            dst[c * LANES:(c + 1) * LANES, :] = blk.astype(dst.dtype)


def _peer_route(h2, wq, sk, tb):
    t, d = h2.shape
    tile = lambda i: (i, 0)
    one = pl.Buffered(1)
    n_slab = tb // LANES
    vregs = lambda n: pltpu.VMEM((n, n_slab, LANES), F32)
    return pl.pallas_call(
        functools.partial(_route_kernel, tb=tb),
        grid=(t // tb,),
        in_specs=[pl.BlockSpec((tb, d), tile),
                  pl.BlockSpec(wq.shape, lambda i: (0, 0, 0), pipeline_mode=one),
                  pl.BlockSpec(sk.shape, lambda i: (0, 0, 0), pipeline_mode=one)],
        out_specs=[pl.BlockSpec((tb, N_PAIRS), tile)] * 3,
        out_shape=[jax.ShapeDtypeStruct((t, N_PAIRS), I32),
                   jax.ShapeDtypeStruct((t, N_PAIRS), I32),
                   jax.ShapeDtypeStruct((t, N_PAIRS), F32)],
        scratch_shapes=[pltpu.VMEM((n_slab * ROUTE_PITCH, LANES), F32),
                        vregs(N_KEYS),
                        pltpu.VMEM((2, PEER_TOPK, n_slab, LANES), F32),
                        pltpu.VMEM((2, PEER_TOPK, n_slab, LANES), F32),
                        vregs(N_PAIRS), vregs(N_PAIRS), vregs(N_PAIRS)],
        compiler_params=pltpu.CompilerParams(
            dimension_semantics=("parallel",), vmem_limit_bytes=VMEM_LIMIT),
        name="peer_route",
    )(h2, wq, sk)


def _ffn_kernel(h2_ref, isel_ref, jsel_ref, g_ref, dnt_ref, up_ref, out_ref, w_ref, *, tb, ec):
    e = pl.program_id(1)

    @pl.when(e == 0)
    def _build():
        out_ref[...] = jnp.zeros_like(out_ref)
        key = lax.broadcasted_iota(I32, (N_KEYS, N_PAIRS), 0)

        def tok(t, _):
            ii = jnp.broadcast_to(isel_ref[pl.ds(t, 1), :], (N_KEYS, N_PAIRS))
            jj = jnp.broadcast_to(jsel_ref[pl.ds(t, 1), :], (N_KEYS, N_PAIRS))
            gg = jnp.broadcast_to(g_ref[pl.ds(t, 1), :], (N_KEYS, N_PAIRS))
            a = jnp.where(ii == key, 1.0, 0.0).astype(BF16)
            b = jnp.where(jj == key, gg, 0.0).astype(BF16)
            w_ref[pl.ds(pl.multiple_of(t * W_PITCH, 8), N_KEYS), :] = lax.dot_general(
                a, b, (((1,), (1,)), ((), ())), preferred_element_type=F32)
            return 0

        lax.fori_loop(0, tb, tok, 0, unroll=16)

    h2 = h2_ref[...]
    tile0 = e * (ec // LANES)
    for c in range(ec // MXU_DIM):
        cols = slice(c * MXU_DIM, (c + 1) * MXU_DIM)
        pre = jnp.dot(h2, dnt_ref[:, cols], preferred_element_type=F32)
        w = jnp.concatenate(
            [w_ref[pl.ds(tile0 + c * (MXU_DIM // LANES) + u, tb, stride=W_PITCH), :]
             for u in range(MXU_DIM // LANES)], axis=-1)
        z = (w * _gelu(pre)).astype(BF16)
        out_ref[...] += jnp.dot(z, up_ref[cols, :], preferred_element_type=F32)


def _peer_ffn(h2, isel, jsel, g, down_t, up, tb, ec):
    t, d = h2.shape
    n_exp = up.shape[0]
    tile = lambda i, e: (i, 0)
    return pl.pallas_call(
        functools.partial(_ffn_kernel, tb=tb, ec=ec),
        grid=(t // tb, n_exp // ec),
        in_specs=[pl.BlockSpec((tb, d), tile),
                  pl.BlockSpec((tb, N_PAIRS), tile),
                  pl.BlockSpec((tb, N_PAIRS), tile),
                  pl.BlockSpec((tb, N_PAIRS), tile),
                  pl.BlockSpec((d, ec), lambda i, e: (0, e)),
                  pl.BlockSpec((ec, d), lambda i, e: (e, 0))],
        out_specs=pl.BlockSpec((tb, d), tile),
        out_shape=jax.ShapeDtypeStruct((t, d), F32),
        scratch_shapes=[pltpu.VMEM((tb * W_PITCH, LANES), F32)],
        compiler_params=pltpu.CompilerParams(
            dimension_semantics=("parallel", "arbitrary"), vmem_limit_bytes=VMEM_LIMIT),
        name="peer_ffn",
    )(h2, isel, jsel, g, down_t, up)


def _finish_kernel(x1_ref, y_ref, gate_ref, fg_ref, o_ref):
    x2 = x1_ref[0] + gate_ref[0] * y_ref[0]
    o_ref[0] = _rms(x2) * fg_ref[...]


def _finish(x1, y, gate_f, final_gain, tm):
    bsz, s, d = x1.shape
    tile = lambda b, i: (b, i, 0)
    return pl.pallas_call(
        _finish_kernel,
        grid=(bsz, s // tm),
        in_specs=[pl.BlockSpec((1, tm, d), tile),
                  pl.BlockSpec((1, tm, d), tile),
                  pl.BlockSpec((1, 1, d), lambda b, i: (b, 0, 0)),
                  pl.BlockSpec((1, d), lambda b, i: (0, 0))],
        out_specs=pl.BlockSpec((1, tm, d), tile),
        out_shape=jax.ShapeDtypeStruct((bsz, s, d), F32),
        compiler_params=pltpu.CompilerParams(
            dimension_semantics=("parallel", "parallel"), vmem_limit_bytes=VMEM_LIMIT),
        name="finish",
    )(x1, y, gate_f, final_gain)


def _layer(x, posf, invf, mod, l, p, tm, tq, tb_route, tb_ffn, ec):
    bsz, s, d = x.shape
    lam_init = 0.8 - 0.6 * math.exp(-0.3 * l)
    qt, k, vt, xp, gates = _mixer_in(
        x, posf, mod, p['norm_mix_gain'][l][None], invf, p['w_in'][l].astype(BF16),
        p['w_gate'][l].astype(BF16), p['b_gate'][l][None], tm)
    lam_vecs = jnp.stack([p['lambda_q1'][l], p['lambda_k1'][l],
                          p['lambda_q2'][l], p['lambda_k2'][l]])
    o = _diff_attn(lam_vecs, qt, k, vt, p['attn_subln_gain'][l][None], lam_init, tq, 4)
    x1, h2 = _merge_out(
        x, xp, o, gates, mod, p['norm_ffn_gain'][l][None], p['pool_w'][l].astype(BF16),
        p['pool_scale'][l][None], p['w_branch_pool'][l].astype(BF16),
        p['w_branch_attn'][l].astype(BF16), p['w_out'][l].astype(BF16), tm)
    h2f = h2.reshape(bsz * s, d)
    wq = p['peer_w_query'][l].astype(BF16).reshape(d, PEER_HEADS, 2 * PEER_HALF).transpose(1, 0, 2)
    isel, jsel, g = _peer_route(h2f, wq, p['peer_sub_keys'][l].astype(BF16), tb_route)
    return h2f, isel, jsel, g, x1


def kernel(x, c, positions, norm_mix_gain, norm_ffn_gain, w_ada, b_ada, w_in, w_gate, b_gate,
           pool_w, pool_scale, lambda_q1, lambda_k1, lambda_q2, lambda_k2, attn_subln_gain,
           w_branch_pool, w_branch_attn, w_out, peer_w_query, peer_sub_keys, peer_down, peer_up,
           final_norm_gain):
    bsz, s, d = x.shape
    depth = w_in.shape[0]
    assert depth == 1, "the driver below runs one layer and then the final RMSNorm"
    p = dict(norm_mix_gain=norm_mix_gain, norm_ffn_gain=norm_ffn_gain, w_in=w_in, w_gate=w_gate,
             b_gate=b_gate, pool_w=pool_w, pool_scale=pool_scale, lambda_q1=lambda_q1,
             lambda_k1=lambda_k1, lambda_q2=lambda_q2, lambda_k2=lambda_k2,
             attn_subln_gain=attn_subln_gain, w_branch_pool=w_branch_pool,
             w_branch_attn=w_branch_attn, w_out=w_out, peer_w_query=peer_w_query,
             peer_sub_keys=peer_sub_keys)
    tm = min(256, s)
    tq = min(256, s)
    tb_route = 1024
    tb_ffn = 512
    ec = 1024
    inv_freq = ROPE_THETA ** (-jnp.arange(0, ROT_DIM, 2, dtype=F32) / ROT_DIM)
    invf = jnp.tile(inv_freq, LANES // (ROT_DIM // 2))[None]
    posf = positions.astype(F32)[..., None]
    l = 0
    mod = _ada_mod(c, w_ada[l], b_ada[l]).reshape(bsz, N_MOD, d)
    h2f, isel, jsel, g, x1 = _layer(x, posf, invf, mod, l, p, tm, tq, tb_route, tb_ffn, ec)
    down_t = peer_down[l].astype(BF16).T
    y = _peer_ffn(h2f, isel, jsel, g, down_t, peer_up[l].astype(BF16), tb_ffn, ec)
    return _finish(x1, y.reshape(bsz, s, d), mod[:, 5:6, :], final_norm_gain[None], tm)
```

```python
import functools
import math

import jax
import jax.numpy as jnp
from jax import lax
from jax.experimental import pallas as pl
from jax.experimental.pallas import tpu as pltpu

F32 = jnp.float32
BF16 = jnp.bfloat16
I32 = jnp.int32

EPS = 1e-6
NEG_INF = -1e30

N_ATTN_HEADS = 8
ATTN_HEAD_DIM = 64
ATTN_V_DIM = 128
ROT_DIM = 16
ROPE_THETA = 500000.0
POOL_WINDOWS = (2, 4, 8, 16)
POOL_GROUP_DIM = 256
POOL_HALO = 16
N_MOD = 6

PEER_HEADS = 8
N_KEYS = 128
PEER_HALF = 128
PEER_TOPK = 16
N_PAIRS = PEER_HEADS * PEER_TOPK

LANES = 128
MXU_DIM = 256
VMEM_LIMIT = 56 * 1024 * 1024

W_PITCH = 136
ROUTE_PITCH = 136


def _rms(x, eps=EPS):
    return x * lax.rsqrt(jnp.mean(x * x, axis=-1, keepdims=True) + eps)


def _gelu(x):
    return 0.5 * x * (1.0 + lax.erf(x * math.sqrt(0.5)))


def _ada_kernel(c_ref, w_ref, b_ref, o_ref):
    ca = jax.nn.silu(c_ref[...])
    o_ref[...] = jnp.dot(ca, w_ref[...], preferred_element_type=F32,
                         precision=lax.Precision.HIGHEST) + b_ref[...]


def _ada_mod(c, w, b):
    bsz, d = c.shape
    n = w.shape[1]
    tn = 1536
    return pl.pallas_call(
        _ada_kernel,
        grid=(n // tn,),
        in_specs=[pl.BlockSpec((bsz, d), lambda j: (0, 0)),
                  pl.BlockSpec((d, tn), lambda j: (0, j)),
                  pl.BlockSpec((1, tn), lambda j: (0, j))],
        out_specs=pl.BlockSpec((bsz, tn), lambda j: (0, j)),
        out_shape=jax.ShapeDtypeStruct((bsz, n), F32),
        compiler_params=pltpu.CompilerParams(vmem_limit_bytes=VMEM_LIMIT),
        name="ada_mod",
    )(c, w, b.reshape(1, n))


def _mixer_in_kernel(x_ref, pos_ref, mod_ref, gain_ref, invf_ref, win_ref, wg_ref, bg_ref,
                     qt_ref, k_ref, vt_ref, xp_ref, gt_ref):
    d = x_ref.shape[-1]
    x = x_ref[0]
    shift = mod_ref[0, 0:1, :]
    scale = mod_ref[0, 1:2, :]
    h = (_rms(x) * gain_ref[...]) * (1.0 + scale) + shift
    hb = h.astype(BF16)

    ang = pos_ref[0] * invf_ref[...]
    cos = jnp.cos(ang)
    sin = jnp.sin(ang)
    lane = lax.broadcasted_iota(I32, (1, LANES), 1) % ATTN_HEAD_DIM
    half = ROT_DIM // 2
    cm = jnp.where(lane < ROT_DIM, cos, 1.0)
    sa = jnp.where(lane < half, -sin, 0.0)
    sb = jnp.where((lane >= half) & (lane < ROT_DIM), sin, 0.0)

    def rope(t):
        return (t * cm + pltpu.roll(t, LANES - half, 1) * sa + pltpu.roll(t, half, 1) * sb)

    p = jnp.dot(hb, win_ref[:, 0:d], preferred_element_type=F32)
    for hh in range(d // LANES):
        sl = slice(hh * LANES, (hh + 1) * LANES)
        qt_ref[0, sl, :] = (rope(p[:, sl]) * ATTN_HEAD_DIM ** -0.5).T.astype(qt_ref.dtype)
    p = jnp.dot(hb, win_ref[:, d:2 * d], preferred_element_type=F32)
    for hh in range(d // LANES):
        sl = slice(hh * LANES, (hh + 1) * LANES)
        k_ref[0, :, sl] = rope(p[:, sl]).astype(k_ref.dtype)
    p = jnp.dot(hb, win_ref[:, 2 * d:3 * d], preferred_element_type=F32)
    for hh in range(d // LANES):
        sl = slice(hh * LANES, (hh + 1) * LANES)
        vt_ref[0, sl, :] = p[:, sl].T.astype(vt_ref.dtype)
    xp_ref[0] = jnp.dot(hb, win_ref[:, 3 * d:4 * d], preferred_element_type=F32)
    for c in range(2):
        g = jnp.dot(hb, wg_ref[:, c * d:(c + 1) * d], preferred_element_type=F32)
        gt_ref[0, :, c * d:(c + 1) * d] = jax.nn.sigmoid(
            g + bg_ref[:, c * d:(c + 1) * d]).astype(gt_ref.dtype)


def _mixer_in(x, posf, mod, gain, invf, w_in, w_gate, b_gate, tm):
    bsz, s, d = x.shape
    const = lambda b, i: (0, 0)
    tile = lambda b, i: (b, i, 0)
    ttile = lambda b, i: (b, 0, i)
    one = pl.Buffered(1)
    return pl.pallas_call(
        _mixer_in_kernel,
        grid=(bsz, s // tm),
        in_specs=[pl.BlockSpec((1, tm, d), tile),
                  pl.BlockSpec((1, tm, 1), tile),
                  pl.BlockSpec((1, N_MOD, d), lambda b, i: (b, 0, 0)),
                  pl.BlockSpec((1, d), const),
                  pl.BlockSpec((1, LANES), const),
                  pl.BlockSpec(w_in.shape, const, pipeline_mode=one),
                  pl.BlockSpec(w_gate.shape, const, pipeline_mode=one),
                  pl.BlockSpec((1, 2 * d), const)],
        out_specs=[pl.BlockSpec((1, d, tm), ttile), pl.BlockSpec((1, tm, d), tile),
                   pl.BlockSpec((1, d, tm), ttile), pl.BlockSpec((1, tm, d), tile),
                   pl.BlockSpec((1, tm, 2 * d), tile)],
        out_shape=[jax.ShapeDtypeStruct((bsz, d, s), BF16),
                   jax.ShapeDtypeStruct((bsz, s, d), BF16),
                   jax.ShapeDtypeStruct((bsz, d, s), BF16),
                   jax.ShapeDtypeStruct((bsz, s, d), F32),
                   jax.ShapeDtypeStruct((bsz, s, 2 * d), BF16)],
        compiler_params=pltpu.CompilerParams(
            dimension_semantics=("parallel", "parallel"), vmem_limit_bytes=VMEM_LIMIT),
        name="mixer_in",
    )(x, posf, mod, gain, invf, w_in, w_gate, b_gate)


def _attn_kernel(lam_ref, qt_ref, k_ref, vt_ref, gain_ref, o_ref, *, nblk, tq, hps, lam_init):
    sub = lax.broadcasted_iota(I32, (LANES, tq), 0)
    zero = jnp.zeros((LANES, tq), BF16)
    qq = []
    for h in range(hps):
        qt = qt_ref[0, h * LANES:(h + 1) * LANES, :]
        qq.append(jnp.concatenate([jnp.where(sub < ATTN_HEAD_DIM, qt, zero),
                                   jnp.where(sub >= ATTN_HEAD_DIM, qt, zero)], axis=1))
    m = [jnp.full((1, 2 * tq), NEG_INF, F32)] * hps
    l = [jnp.zeros((1, 2 * tq), F32)] * hps
    acc = [jnp.zeros((ATTN_V_DIM, 2 * tq), F32)] * hps

    def scores(j, h):
        return jnp.dot(k_ref[0, j * tq:(j + 1) * tq, h * LANES:(h + 1) * LANES], qq[h],
                       preferred_element_type=F32)

    s_next = [scores(0, h) for h in range(hps)]
    for j in range(nblk):
        for h in range(hps):
            s = s_next[h]
            if j + 1 < nblk:
                s_next[h] = scores(j + 1, h)
            if j == nblk - 1:
                kv = lax.broadcasted_iota(I32, s.shape, 0)
                qr = lax.broadcasted_iota(I32, s.shape, 1) % tq
                s = jnp.where(qr >= kv, s, NEG_INF)
            m_new = jnp.maximum(m[h], jnp.max(s, axis=0, keepdims=True))
            alpha = jnp.exp(m[h] - m_new)
            p = jnp.exp(s - m_new)
            l[h] = alpha * l[h] + jnp.sum(p, axis=0, keepdims=True)
            vt = vt_ref[0, h * LANES:(h + 1) * LANES, j * tq:(j + 1) * tq]
            acc[h] = alpha * acc[h] + jnp.dot(vt, p.astype(BF16),
                                              preferred_element_type=F32)
            m[h] = m_new

    lv = lam_ref[...]
    lam = (jnp.exp(jnp.sum(lv[0:1] * lv[1:2], axis=-1, keepdims=True))
           - jnp.exp(jnp.sum(lv[2:3] * lv[3:4], axis=-1, keepdims=True)) + lam_init)
    for h in range(hps):
        cols = slice(h * LANES, (h + 1) * LANES)
        ot = acc[h] * (1.0 / l[h])
        o = (ot[:, :tq] - lam * ot[:, tq:]).T
        o_ref[0, :, cols] = (_rms(o) * gain_ref[:, cols] * (1.0 - lam_init)).astype(o_ref.dtype)


def _diff_attn(lam_vecs, qt, k, vt, gain, lam_init, tq, hps):
    bsz, s, d = k.shape
    w = hps * LANES
    outs = []
    for qi in range(s // tq):
        kv_len = (qi + 1) * tq
        outs.append(pl.pallas_call(
            functools.partial(_attn_kernel, nblk=qi + 1, tq=tq, hps=hps, lam_init=lam_init),
            grid=(bsz, d // w),
            in_specs=[pl.BlockSpec(lam_vecs.shape, lambda b, h: (0, 0)),
                      pl.BlockSpec((1, w, tq), lambda b, h, qi=qi: (b, h, qi)),
                      pl.BlockSpec((1, kv_len, w), lambda b, h: (b, 0, h)),
                      pl.BlockSpec((1, w, kv_len), lambda b, h: (b, h, 0)),
                      pl.BlockSpec((1, w), lambda b, h: (0, h))],
            out_specs=pl.BlockSpec((1, tq, w), lambda b, h: (b, 0, h)),
            out_shape=jax.ShapeDtypeStruct((bsz, tq, d), BF16),
            compiler_params=pltpu.CompilerParams(
                dimension_semantics=("parallel", "parallel"), vmem_limit_bytes=VMEM_LIMIT),
            name=f"diff_attn_q{qi}",
        )(lam_vecs, qt, k, vt, gain))
    return jnp.concatenate(outs, axis=1)


def _merge_kernel(x_ref, xp_ref, halo_ref, o_ref, gt_ref, mod_ref, gain_ref, pw_ref, ps_ref,
                  wbp_ref, wba_ref, wout_ref, x1_ref, h2_ref, xs_ref, *, tm):
    i = pl.program_id(1)
    d = x_ref.shape[-1]
    halo = halo_ref[0]
    xs_ref[0:POOL_HALO, :] = jnp.where(i > 0, halo, jnp.zeros_like(halo))
    xs_ref[POOL_HALO:, :] = xp_ref[0]
    pos = i * tm + lax.broadcasted_iota(I32, (tm, 1), 0)
    y_pool = None
    ya = []
    for g, w in enumerate(POOL_WINDOWS):
        cols = slice(g * POOL_GROUP_DIM, (g + 1) * POOL_GROUP_DIM)
        tok = xs_ref[POOL_HALO:, cols]
        acc = tok
        for u in range(1, w):
            acc = acc + xs_ref[POOL_HALO - u:POOL_HALO - u + tm, cols]
        cnt = jnp.minimum(pos + 1, w).astype(F32)
        pooled = acc / cnt - tok
        yg = jnp.dot(pooled.astype(BF16), pw_ref[g], preferred_element_type=F32)
        ya.append((yg * ps_ref[:, cols]).astype(BF16))
    ya = jnp.concatenate(ya, axis=-1)
    y_pool = jnp.dot(ya, wbp_ref[...], preferred_element_type=F32)
    y_attn = jnp.dot(o_ref[0], wba_ref[...], preferred_element_type=F32)
    merged = (gt_ref[0, :, 0:d].astype(F32) * y_pool + gt_ref[0, :, d:2 * d].astype(F32) * y_attn)
    mix = jnp.dot(merged.astype(BF16), wout_ref[...], preferred_element_type=F32)
    x1 = x_ref[0] + mod_ref[0, 2:3, :] * mix
    x1_ref[0] = x1
    h2 = (_rms(x1) * gain_ref[...]) * (1.0 + mod_ref[0, 4:5, :]) + mod_ref[0, 3:4, :]
    h2_ref[0] = h2.astype(h2_ref.dtype)


def _merge_out(x, xp, o, gates, mod, gain_f, pool_w, pool_scale, wbp, wba, wout, tm):
    bsz, s, d = x.shape
    const2 = lambda b, i: (0, 0)
    tile = lambda b, i: (b, i, 0)
    hb = tm // POOL_HALO
    one = pl.Buffered(1)
    return pl.pallas_call(
        functools.partial(_merge_kernel, tm=tm),
        grid=(bsz, s // tm),
        in_specs=[pl.BlockSpec((1, tm, d), tile),
                  pl.BlockSpec((1, tm, d), tile),
                  pl.BlockSpec((1, POOL_HALO, d), lambda b, i: (b, jnp.maximum(i * hb - 1, 0), 0)),
                  pl.BlockSpec((1, tm, d), tile),
                  pl.BlockSpec((1, tm, 2 * d), tile),
                  pl.BlockSpec((1, N_MOD, d), lambda b, i: (b, 0, 0)),
                  pl.BlockSpec((1, d), const2),
                  pl.BlockSpec(pool_w.shape, lambda b, i: (0, 0, 0), pipeline_mode=one),
                  pl.BlockSpec((1, d), const2),
                  pl.BlockSpec(wbp.shape, const2, pipeline_mode=one),
                  pl.BlockSpec(wba.shape, const2, pipeline_mode=one),
                  pl.BlockSpec(wout.shape, const2, pipeline_mode=one)],
        out_specs=[pl.BlockSpec((1, tm, d), tile), pl.BlockSpec((1, tm, d), tile)],
        out_shape=[jax.ShapeDtypeStruct((bsz, s, d), F32),
                   jax.ShapeDtypeStruct((bsz, s, d), BF16)],
        scratch_shapes=[pltpu.VMEM((tm + POOL_HALO, d), F32)],
        compiler_params=pltpu.CompilerParams(
            dimension_semantics=("parallel", "parallel"), vmem_limit_bytes=VMEM_LIMIT),
        name="merge_out",
    )(x, xp, xp, o, gates, mod, gain_f, pool_w, pool_scale, wbp, wba, wout)


def _staircase():
    return [(a, b) for a in range(PEER_TOPK) for b in range(PEER_TOPK)
            if (a + 1) * (b + 1) <= PEER_TOPK]


def _tree(op, xs):
    xs = list(xs)
    while len(xs) > 1:
        xs = [op(xs[i], xs[i + 1]) if i + 1 < len(xs) else xs[i] for i in range(0, len(xs), 2)]
    return xs[0]


def _route_kernel(h2_ref, wq_ref, sk_ref, isel_ref, jsel_ref, g_ref,
                  slab_ref, km_ref, sv_ref, si_ref, oi_ref, oj_ref, og_ref, *, tb):
    n_slab = tb // LANES
    big = 1e9
    h2 = h2_ref[...]
    cands = _staircase()

    def head(hh, _):
        q = jnp.dot(h2, wq_ref[hh], preferred_element_type=F32).astype(BF16)
        for half in range(2):
            qh = q[:, half * PEER_HALF:(half + 1) * PEER_HALF]
            st = lax.dot_general(sk_ref[half], qh, (((1,), (1,)), ((), ())),
                                 preferred_element_type=F32)
            for c in range(n_slab):
                slab_ref[c * ROUTE_PITCH:c * ROUTE_PITCH + N_KEYS, :] = (
                    st[:, c * LANES:(c + 1) * LANES])
            for k in range(N_KEYS):
                km_ref[k] = slab_ref[pl.ds(k, n_slab, stride=ROUTE_PITCH), :]

            def extract(r, pos_prev):
                part = [None] * 8
                for k in range(N_KEYS):
                    v = jnp.where(pos_prev == float(k), -jnp.inf, km_ref[k])
                    km_ref[k] = v
                    part[k % 8] = v if part[k % 8] is None else jnp.maximum(part[k % 8], v)
                m = _tree(jnp.maximum, part)
                part = [None] * 8
                for k in range(N_KEYS):
                    c = jnp.where(km_ref[k] == m, float(k), big)
                    part[k % 8] = c if part[k % 8] is None else jnp.minimum(part[k % 8], c)
                pos = _tree(jnp.minimum, part)
                sv_ref[half, r] = m
                si_ref[half, r] = pos
                return pos

            lax.fori_loop(0, PEER_TOPK, extract, jnp.full((n_slab, LANES), -1.0, F32))

        sv0 = [sv_ref[0, a] for a in range(PEER_TOPK)]
        sv1 = [sv_ref[1, b] for b in range(PEER_TOPK)]
        cand = [sv0[a] + sv1[b] for a, b in cands]
        code = [float(a * PEER_TOPK + b) for a, b in cands]
        top_s, top_pos = [], []
        for _ in range(PEER_TOPK):
            m = _tree(jnp.maximum, cand)
            pos = _tree(jnp.minimum, [jnp.where(c == m, cd, big) for c, cd in zip(cand, code)])
            top_s.append(m)
            top_pos.append(pos)
            cand = [jnp.where(pos == cd, -jnp.inf, c) for c, cd in zip(cand, code)]
        si0 = [si_ref[0, a] for a in range(PEER_TOPK)]
        si1 = [si_ref[1, b] for b in range(PEER_TOPK)]
        e = [jnp.exp(t - top_s[0]) for t in top_s]
        denom = _tree(jnp.add, e)
        for r in range(PEER_TOPK):
            a_sel = jnp.floor(top_pos[r] * (1.0 / PEER_TOPK))
            b_sel = top_pos[r] - a_sel * PEER_TOPK
            i_sel = si0[0]
            j_sel = si1[0]
            for a in range(1, PEER_TOPK):
                i_sel = jnp.where(a_sel == float(a), si0[a], i_sel)
                j_sel = jnp.where(b_sel == float(a), si1[a], j_sel)
            row = hh * PEER_TOPK + r
            oi_ref[row] = i_sel
            oj_ref[row] = j_sel
            og_ref[row] = e[r] / denom
        return 0

    lax.fori_loop(0, PEER_HEADS, head, 0)

    for src, dst in ((oi_ref, isel_ref), (oj_ref, jsel_ref), (og_ref, g_ref)):
        for p in range(N_PAIRS):
            slab_ref[pl.ds(p, n_slab, stride=ROUTE_PITCH), :] = src[p]
        for c in range(n_slab):
            blk = slab_ref[c * ROUTE_PITCH:c * ROUTE_PITCH + N_PAIRS, :].T
            dst[c * LANES:(c + 1) * LANES, :] = blk.astype(dst.dtype)


def _peer_route(h2, wq, sk, tb):
    t, d = h2.shape
    tile = lambda i: (i, 0)
    one = pl.Buffered(1)
    n_slab = tb // LANES
    vregs = lambda n: pltpu.VMEM((n, n_slab, LANES), F32)
    return pl.pallas_call(
        functools.partial(_route_kernel, tb=tb),
        grid=(t // tb,),
        in_specs=[pl.BlockSpec((tb, d), tile),
                  pl.BlockSpec(wq.shape, lambda i: (0, 0, 0), pipeline_mode=one),
                  pl.BlockSpec(sk.shape, lambda i: (0, 0, 0), pipeline_mode=one)],
        out_specs=[pl.BlockSpec((tb, N_PAIRS), tile)] * 3,
        out_shape=[jax.ShapeDtypeStruct((t, N_PAIRS), I32),
                   jax.ShapeDtypeStruct((t, N_PAIRS), I32),
                   jax.ShapeDtypeStruct((t, N_PAIRS), F32)],
        scratch_shapes=[pltpu.VMEM((n_slab * ROUTE_PITCH, LANES), F32),
                        vregs(N_KEYS),
                        pltpu.VMEM((2, PEER_TOPK, n_slab, LANES), F32),
                        pltpu.VMEM((2, PEER_TOPK, n_slab, LANES), F32),
                        vregs(N_PAIRS), vregs(N_PAIRS), vregs(N_PAIRS)],
        compiler_params=pltpu.CompilerParams(
            dimension_semantics=("parallel",), vmem_limit_bytes=VMEM_LIMIT),
        name="peer_route",
    )(h2, wq, sk)


def _ffn_kernel(h2_ref, isel_ref, jsel_ref, g_ref, dnt_ref, up_ref, out_ref, w_ref, *, tb, ec):
    e = pl.program_id(1)

    @pl.when(e == 0)
    def _build():
        out_ref[...] = jnp.zeros_like(out_ref)
        key = lax.broadcasted_iota(I32, (N_KEYS, N_PAIRS), 0)

        def tok(t, _):
            ii = jnp.broadcast_to(isel_ref[pl.ds(t, 1), :], (N_KEYS, N_PAIRS))
            jj = jnp.broadcast_to(jsel_ref[pl.ds(t, 1), :], (N_KEYS, N_PAIRS))
            gg = jnp.broadcast_to(g_ref[pl.ds(t, 1), :], (N_KEYS, N_PAIRS))
            a = jnp.where(ii == key, 1.0, 0.0).astype(BF16)
            b = jnp.where(jj == key, gg, 0.0).astype(BF16)
            w_ref[pl.ds(pl.multiple_of(t * W_PITCH, 8), N_KEYS), :] = lax.dot_general(
                a, b, (((1,), (1,)), ((), ())), preferred_element_type=F32)
            return 0

        lax.fori_loop(0, tb, tok, 0, unroll=16)

    h2 = h2_ref[...]
    tile0 = e * (ec // LANES)
    z = []
    for c in range(ec // MXU_DIM):
        cols = slice(c * MXU_DIM, (c + 1) * MXU_DIM)
        pre = jnp.dot(h2, dnt_ref[:, cols], preferred_element_type=F32)
        w = jnp.concatenate(
            [w_ref[pl.ds(tile0 + c * (MXU_DIM // LANES) + u, tb, stride=W_PITCH), :]
             for u in range(MXU_DIM // LANES)], axis=-1)
        z.append((w * _gelu(pre)).astype(BF16))
    out_ref[...] += jnp.dot(jnp.concatenate(z, axis=-1), up_ref[...],
                            preferred_element_type=F32)


def _peer_ffn(h2, isel, jsel, g, down_t, up, tb, ec):
    t, d = h2.shape
    n_exp = up.shape[0]
    tile = lambda i, e: (i, 0)
    return pl.pallas_call(
        functools.partial(_ffn_kernel, tb=tb, ec=ec),
        grid=(t // tb, n_exp // ec),
        in_specs=[pl.BlockSpec((tb, d), tile),
                  pl.BlockSpec((tb, N_PAIRS), tile),
                  pl.BlockSpec((tb, N_PAIRS), tile),
                  pl.BlockSpec((tb, N_PAIRS), tile),
                  pl.BlockSpec((d, ec), lambda i, e: (0, e)),
                  pl.BlockSpec((ec, d), lambda i, e: (e, 0))],
        out_specs=pl.BlockSpec((tb, d), tile),
        out_shape=jax.ShapeDtypeStruct((t, d), F32),
        scratch_shapes=[pltpu.VMEM((tb * W_PITCH, LANES), F32)],
        compiler_params=pltpu.CompilerParams(
            dimension_semantics=("parallel", "arbitrary"), vmem_limit_bytes=VMEM_LIMIT),
        name="peer_ffn",
    )(h2, isel, jsel, g, down_t, up)


def _finish_kernel(x1_ref, y_ref, gate_ref, fg_ref, o_ref):
    x2 = x1_ref[0] + gate_ref[0] * y_ref[0]
    o_ref[0] = _rms(x2) * fg_ref[...]


def _finish(x1, y, gate_f, final_gain, tm):
    bsz, s, d = x1.shape
    tile = lambda b, i: (b, i, 0)
    return pl.pallas_call(
        _finish_kernel,
        grid=(bsz, s // tm),
        in_specs=[pl.BlockSpec((1, tm, d), tile),
                  pl.BlockSpec((1, tm, d), tile),
                  pl.BlockSpec((1, 1, d), lambda b, i: (b, 0, 0)),
                  pl.BlockSpec((1, d), lambda b, i: (0, 0))],
        out_specs=pl.BlockSpec((1, tm, d), tile),
        out_shape=jax.ShapeDtypeStruct((bsz, s, d), F32),
        compiler_params=pltpu.CompilerParams(
            dimension_semantics=("parallel", "parallel"), vmem_limit_bytes=VMEM_LIMIT),
        name="finish",
    )(x1, y, gate_f, final_gain)


def _layer(x, posf, invf, mod, l, p, tm, tq, tb_route, tb_ffn, ec):
    bsz, s, d = x.shape
    lam_init = 0.8 - 0.6 * math.exp(-0.3 * l)
    qt, k, vt, xp, gates = _mixer_in(
        x, posf, mod, p['norm_mix_gain'][l][None], invf, p['w_in'][l].astype(BF16),
        p['w_gate'][l].astype(BF16), p['b_gate'][l][None], tm)
    lam_vecs = jnp.stack([p['lambda_q1'][l], p['lambda_k1'][l],
                          p['lambda_q2'][l], p['lambda_k2'][l]])
    o = _diff_attn(lam_vecs, qt, k, vt, p['attn_subln_gain'][l][None], lam_init, tq, 4)
    x1, h2 = _merge_out(
        x, xp, o, gates, mod, p['norm_ffn_gain'][l][None], p['pool_w'][l].astype(BF16),
        p['pool_scale'][l][None], p['w_branch_pool'][l].astype(BF16),
        p['w_branch_attn'][l].astype(BF16), p['w_out'][l].astype(BF16), tm)
    h2f = h2.reshape(bsz * s, d)
    wq = p['peer_w_query'][l].astype(BF16).reshape(d, PEER_HEADS, 2 * PEER_HALF).transpose(1, 0, 2)
    isel, jsel, g = _peer_route(h2f, wq, p['peer_sub_keys'][l].astype(BF16), tb_route)
    return h2f, isel, jsel, g, x1


def kernel(x, c, positions, norm_mix_gain, norm_ffn_gain, w_ada, b_ada, w_in, w_gate, b_gate,
           pool_w, pool_scale, lambda_q1, lambda_k1, lambda_q2, lambda_k2, attn_subln_gain,
           w_branch_pool, w_branch_attn, w_out, peer_w_query, peer_sub_keys, peer_down, peer_up,
           final_norm_gain):
    bsz, s, d = x.shape
    depth = w_in.shape[0]
    assert depth == 1, "the driver below runs one layer and then the final RMSNorm"
    p = dict(norm_mix_gain=norm_mix_gain, norm_ffn_gain=norm_ffn_gain, w_in=w_in, w_gate=w_gate,
             b_gate=b_gate, pool_w=pool_w, pool_scale=pool_scale, lambda_q1=lambda_q1,
             lambda_k1=lambda_k1, lambda_q2=lambda_q2, lambda_k2=lambda_k2,
             attn_subln_gain=attn_subln_gain, w_branch_pool=w_branch_pool,
             w_branch_attn=w_branch_attn, w_out=w_out, peer_w_query=peer_w_query,
             peer_sub_keys=peer_sub_keys)
    tm = min(256, s)
    tq = min(256, s)
    tb_route = 1024
    tb_ffn = 512
    ec = 1024
    inv_freq = ROPE_THETA ** (-jnp.arange(0, ROT_DIM, 2, dtype=F32) / ROT_DIM)
    invf = jnp.tile(inv_freq, LANES // (ROT_DIM // 2))[None]
    posf = positions.astype(F32)[..., None]
    l = 0
    mod = _ada_mod(c, w_ada[l], b_ada[l]).reshape(bsz, N_MOD, d)
    h2f, isel, jsel, g, x1 = _layer(x, posf, invf, mod, l, p, tm, tq, tb_route, tb_ffn, ec)
    down_t = peer_down[l].astype(BF16).T
    y = _peer_ffn(h2f, isel, jsel, g, down_t, peer_up[l].astype(BF16), tb_ffn, ec)
    return _finish(x1, y.reshape(bsz, s, d), mod[:, 5:6, :], final_norm_gain[None], tm)
```

```python
import functools
import math

import jax
import jax.numpy as jnp
from jax import lax
from jax.experimental import pallas as pl
from jax.experimental.pallas import tpu as pltpu

F32 = jnp.float32
BF16 = jnp.bfloat16
I32 = jnp.int32

EPS = 1e-6
NEG_INF = -1e30

N_ATTN_HEADS = 8
ATTN_HEAD_DIM = 64
ATTN_V_DIM = 128
ROT_DIM = 16
ROPE_THETA = 500000.0
POOL_WINDOWS = (2, 4, 8, 16)
POOL_GROUP_DIM = 256
POOL_HALO = 16
N_MOD = 6

PEER_HEADS = 8
N_KEYS = 128
PEER_HALF = 128
PEER_TOPK = 16
N_PAIRS = PEER_HEADS * PEER_TOPK

LANES = 128
MXU_DIM = 256
VMEM_LIMIT = 56 * 1024 * 1024

W_PITCH = 136
ROUTE_PITCH = 136


def _rms(x, eps=EPS):
    return x * lax.rsqrt(jnp.mean(x * x, axis=-1, keepdims=True) + eps)


def _gelu(x):
    return 0.5 * x * (1.0 + lax.erf(x * math.sqrt(0.5)))


def _ada_kernel(c_ref, w_ref, b_ref, o_ref):
    ca = jax.nn.silu(c_ref[...])
    o_ref[...] = jnp.dot(ca, w_ref[...], preferred_element_type=F32,
                         precision=lax.Precision.HIGHEST) + b_ref[...]


def _ada_mod(c, w, b):
    bsz, d = c.shape
    n = w.shape[1]
    tn = 1536
    return pl.pallas_call(
        _ada_kernel,
        grid=(n // tn,),
        in_specs=[pl.BlockSpec((bsz, d), lambda j: (0, 0)),
                  pl.BlockSpec((d, tn), lambda j: (0, j)),
                  pl.BlockSpec((1, tn), lambda j: (0, j))],
        out_specs=pl.BlockSpec((bsz, tn), lambda j: (0, j)),
        out_shape=jax.ShapeDtypeStruct((bsz, n), F32),
        compiler_params=pltpu.CompilerParams(vmem_limit_bytes=VMEM_LIMIT),
        name="ada_mod",
    )(c, w, b.reshape(1, n))


def _mixer_in_kernel(x_ref, pos_ref, mod_ref, gain_ref, invf_ref, win_ref, wg_ref, bg_ref,
                     qt_ref, k_ref, vt_ref, xp_ref, gt_ref):
    d = x_ref.shape[-1]
    x = x_ref[0]
    shift = mod_ref[0, 0:1, :]
    scale = mod_ref[0, 1:2, :]
    h = (_rms(x) * gain_ref[...]) * (1.0 + scale) + shift
    hb = h.astype(BF16)

    ang = pos_ref[0] * invf_ref[...]
    cos = jnp.cos(ang)
    sin = jnp.sin(ang)
    lane = lax.broadcasted_iota(I32, (1, LANES), 1) % ATTN_HEAD_DIM
    half = ROT_DIM // 2
    cm = jnp.where(lane < ROT_DIM, cos, 1.0)
    sa = jnp.where(lane < half, -sin, 0.0)
    sb = jnp.where((lane >= half) & (lane < ROT_DIM), sin, 0.0)

    def rope(t):
        return (t * cm + pltpu.roll(t, LANES - half, 1) * sa + pltpu.roll(t, half, 1) * sb)

    p = jnp.dot(hb, win_ref[:, 0:d], preferred_element_type=F32)
    for hh in range(d // LANES):
        sl = slice(hh * LANES, (hh + 1) * LANES)
        qt_ref[0, sl, :] = (rope(p[:, sl]) * ATTN_HEAD_DIM ** -0.5).T.astype(qt_ref.dtype)
    p = jnp.dot(hb, win_ref[:, d:2 * d], preferred_element_type=F32)
    for hh in range(d // LANES):
        sl = slice(hh * LANES, (hh + 1) * LANES)
        k_ref[0, :, sl] = rope(p[:, sl]).astype(k_ref.dtype)
    p = jnp.dot(hb, win_ref[:, 2 * d:3 * d], preferred_element_type=F32)
    for hh in range(d // LANES):
        sl = slice(hh * LANES, (hh + 1) * LANES)
        vt_ref[0, sl, :] = p[:, sl].T.astype(vt_ref.dtype)
    xp_ref[0] = jnp.dot(hb, win_ref[:, 3 * d:4 * d], preferred_element_type=F32)
    for c in range(2):
        g = jnp.dot(hb, wg_ref[:, c * d:(c + 1) * d], preferred_element_type=F32)
        gt_ref[0, :, c * d:(c + 1) * d] = jax.nn.sigmoid(
            g + bg_ref[:, c * d:(c + 1) * d]).astype(gt_ref.dtype)


def _mixer_in(x, posf, mod, gain, invf, w_in, w_gate, b_gate, tm):
    bsz, s, d = x.shape
    const = lambda b, i: (0, 0)
    tile = lambda b, i: (b, i, 0)
    ttile = lambda b, i: (b, 0, i)
    one = pl.Buffered(1)
    return pl.pallas_call(
        _mixer_in_kernel,
        grid=(bsz, s // tm),
        in_specs=[pl.BlockSpec((1, tm, d), tile),
                  pl.BlockSpec((1, tm, 1), tile),
                  pl.BlockSpec((1, N_MOD, d), lambda b, i: (b, 0, 0)),
                  pl.BlockSpec((1, d), const),
                  pl.BlockSpec((1, LANES), const),
                  pl.BlockSpec(w_in.shape, const, pipeline_mode=one),
                  pl.BlockSpec(w_gate.shape, const, pipeline_mode=one),
                  pl.BlockSpec((1, 2 * d), const)],
        out_specs=[pl.BlockSpec((1, d, tm), ttile), pl.BlockSpec((1, tm, d), tile),
                   pl.BlockSpec((1, d, tm), ttile), pl.BlockSpec((1, tm, d), tile),
                   pl.BlockSpec((1, tm, 2 * d), tile)],
        out_shape=[jax.ShapeDtypeStruct((bsz, d, s), BF16),
                   jax.ShapeDtypeStruct((bsz, s, d), BF16),
                   jax.ShapeDtypeStruct((bsz, d, s), BF16),
                   jax.ShapeDtypeStruct((bsz, s, d), F32),
                   jax.ShapeDtypeStruct((bsz, s, 2 * d), BF16)],
        compiler_params=pltpu.CompilerParams(
            dimension_semantics=("parallel", "parallel"), vmem_limit_bytes=VMEM_LIMIT),
        name="mixer_in",
    )(x, posf, mod, gain, invf, w_in, w_gate, b_gate)


def _attn_kernel(lam_ref, qt_ref, k_ref, vt_ref, gain_ref, o_ref, *, nblk, tq, hps, lam_init):
    sub = lax.broadcasted_iota(I32, (LANES, tq), 0)
    zero = jnp.zeros((LANES, tq), BF16)
    qq = []
    for h in range(hps):
        qt = qt_ref[0, h * LANES:(h + 1) * LANES, :]
        qq.append(jnp.concatenate([jnp.where(sub < ATTN_HEAD_DIM, qt, zero),
                                   jnp.where(sub >= ATTN_HEAD_DIM, qt, zero)], axis=1))
    m = [jnp.full((1, 2 * tq), NEG_INF, F32)] * hps
    l = [jnp.zeros((1, 2 * tq), F32)] * hps
    acc = [jnp.zeros((ATTN_V_DIM, 2 * tq), F32)] * hps

    def scores(j, h):
        return jnp.dot(k_ref[0, j * tq:(j + 1) * tq, h * LANES:(h + 1) * LANES], qq[h],
                       preferred_element_type=F32)

    s_next = [scores(0, h) for h in range(hps)]
    for j in range(nblk):
        for h in range(hps):
            s = s_next[h]
            if j + 1 < nblk:
                s_next[h] = scores(j + 1, h)
            if j == nblk - 1:
                kv = lax.broadcasted_iota(I32, s.shape, 0)
                qr = lax.broadcasted_iota(I32, s.shape, 1) % tq
                s = jnp.where(qr >= kv, s, NEG_INF)
            m_new = jnp.maximum(m[h], jnp.max(s, axis=0, keepdims=True))
            alpha = jnp.exp(m[h] - m_new)
            p = jnp.exp(s - m_new)
            l[h] = alpha * l[h] + jnp.sum(p, axis=0, keepdims=True)
            vt = vt_ref[0, h * LANES:(h + 1) * LANES, j * tq:(j + 1) * tq]
            acc[h] = alpha * acc[h] + jnp.dot(vt, p.astype(BF16),
                                              preferred_element_type=F32)
            m[h] = m_new

    lv = lam_ref[...]
    lam = (jnp.exp(jnp.sum(lv[0:1] * lv[1:2], axis=-1, keepdims=True))
           - jnp.exp(jnp.sum(lv[2:3] * lv[3:4], axis=-1, keepdims=True)) + lam_init)
    for h in range(hps):
        cols = slice(h * LANES, (h + 1) * LANES)
        ot = acc[h] * (1.0 / l[h])
        o = (ot[:, :tq] - lam * ot[:, tq:]).T
        o_ref[0, :, cols] = (_rms(o) * gain_ref[:, cols] * (1.0 - lam_init)).astype(o_ref.dtype)


def _diff_attn(lam_vecs, qt, k, vt, gain, lam_init, tq, hps):
    bsz, s, d = k.shape
    w = hps * LANES
    outs = []
    for qi in range(s // tq):
        kv_len = (qi + 1) * tq
        outs.append(pl.pallas_call(
            functools.partial(_attn_kernel, nblk=qi + 1, tq=tq, hps=hps, lam_init=lam_init),
            grid=(bsz, d // w),
            in_specs=[pl.BlockSpec(lam_vecs.shape, lambda b, h: (0, 0)),
                      pl.BlockSpec((1, w, tq), lambda b, h, qi=qi: (b, h, qi)),
                      pl.BlockSpec((1, kv_len, w), lambda b, h: (b, 0, h)),
                      pl.BlockSpec((1, w, kv_len), lambda b, h: (b, h, 0)),
                      pl.BlockSpec((1, w), lambda b, h: (0, h))],
            out_specs=pl.BlockSpec((1, tq, w), lambda b, h: (b, 0, h)),
            out_shape=jax.ShapeDtypeStruct((bsz, tq, d), BF16),
            compiler_params=pltpu.CompilerParams(
                dimension_semantics=("parallel", "parallel"), vmem_limit_bytes=VMEM_LIMIT),
            name=f"diff_attn_q{qi}",
        )(lam_vecs, qt, k, vt, gain))
    return jnp.concatenate(outs, axis=1)


def _merge_kernel(x_ref, xp_ref, halo_ref, o_ref, gt_ref, mod_ref, gain_ref, pw_ref, ps_ref,
                  wbp_ref, wba_ref, wout_ref, x1_ref, h2_ref, xs_ref, *, tm):
    i = pl.program_id(1)
    d = x_ref.shape[-1]
    halo = halo_ref[0]
    xs_ref[0:POOL_HALO, :] = jnp.where(i > 0, halo, jnp.zeros_like(halo))
    xs_ref[POOL_HALO:, :] = xp_ref[0]
    pos = i * tm + lax.broadcasted_iota(I32, (tm, 1), 0)
    y_pool = None
    ya = []
    for g, w in enumerate(POOL_WINDOWS):
        cols = slice(g * POOL_GROUP_DIM, (g + 1) * POOL_GROUP_DIM)
        tok = xs_ref[POOL_HALO:, cols]
        acc = tok
        for u in range(1, w):
            acc = acc + xs_ref[POOL_HALO - u:POOL_HALO - u + tm, cols]
        cnt = jnp.minimum(pos + 1, w).astype(F32)
        pooled = acc / cnt - tok
        yg = jnp.dot(pooled.astype(BF16), pw_ref[g], preferred_element_type=F32)
        ya.append((yg * ps_ref[:, cols]).astype(BF16))
    ya = jnp.concatenate(ya, axis=-1)
    y_pool = jnp.dot(ya, wbp_ref[...], preferred_element_type=F32)
    y_attn = jnp.dot(o_ref[0], wba_ref[...], preferred_element_type=F32)
    merged = (gt_ref[0, :, 0:d].astype(F32) * y_pool + gt_ref[0, :, d:2 * d].astype(F32) * y_attn)
    mix = jnp.dot(merged.astype(BF16), wout_ref[...], preferred_element_type=F32)
    x1 = x_ref[0] + mod_ref[0, 2:3, :] * mix
    x1_ref[0] = x1
    h2 = (_rms(x1) * gain_ref[...]) * (1.0 + mod_ref[0, 4:5, :]) + mod_ref[0, 3:4, :]
    h2_ref[0] = h2.astype(h2_ref.dtype)


def _merge_out(x, xp, o, gates, mod, gain_f, pool_w, pool_scale, wbp, wba, wout, tm):
    bsz, s, d = x.shape
    const2 = lambda b, i: (0, 0)
    tile = lambda b, i: (b, i, 0)
    hb = tm // POOL_HALO
    one = pl.Buffered(1)
    return pl.pallas_call(
        functools.partial(_merge_kernel, tm=tm),
        grid=(bsz, s // tm),
        in_specs=[pl.BlockSpec((1, tm, d), tile),
                  pl.BlockSpec((1, tm, d), tile),
                  pl.BlockSpec((1, POOL_HALO, d), lambda b, i: (b, jnp.maximum(i * hb - 1, 0), 0)),
                  pl.BlockSpec((1, tm, d), tile),
                  pl.BlockSpec((1, tm, 2 * d), tile),
                  pl.BlockSpec((1, N_MOD, d), lambda b, i: (b, 0, 0)),
                  pl.BlockSpec((1, d), const2),
                  pl.BlockSpec(pool_w.shape, lambda b, i: (0, 0, 0), pipeline_mode=one),
                  pl.BlockSpec((1, d), const2),
                  pl.BlockSpec(wbp.shape, const2, pipeline_mode=one),
                  pl.BlockSpec(wba.shape, const2, pipeline_mode=one),
                  pl.BlockSpec(wout.shape, const2, pipeline_mode=one)],
        out_specs=[pl.BlockSpec((1, tm, d), tile), pl.BlockSpec((1, tm, d), tile)],
        out_shape=[jax.ShapeDtypeStruct((bsz, s, d), F32),
                   jax.ShapeDtypeStruct((bsz, s, d), BF16)],
        scratch_shapes=[pltpu.VMEM((tm + POOL_HALO, d), F32)],
        compiler_params=pltpu.CompilerParams(
            dimension_semantics=("parallel", "parallel"), vmem_limit_bytes=VMEM_LIMIT),
        name="merge_out",
    )(x, xp, xp, o, gates, mod, gain_f, pool_w, pool_scale, wbp, wba, wout)


def _staircase():
    return [(a, b) for a in range(PEER_TOPK) for b in range(PEER_TOPK)
            if (a + 1) * (b + 1) <= PEER_TOPK]


def _tree(op, xs):
    xs = list(xs)
    while len(xs) > 1:
        xs = [op(xs[i], xs[i + 1]) if i + 1 < len(xs) else xs[i] for i in range(0, len(xs), 2)]
    return xs[0]


def _route_kernel(h2_ref, wq_ref, sk_ref, isel_ref, jsel_ref, g_ref,
                  slab_ref, km_ref, sv_ref, si_ref, oi_ref, oj_ref, og_ref, *, tb):
    n_slab = tb // LANES
    big = 1e9
    h2 = h2_ref[...]
    cands = _staircase()

    def head(hh, _):
        q = jnp.dot(h2, wq_ref[hh], preferred_element_type=F32).astype(BF16)
        for half in range(2):
            qh = q[:, half * PEER_HALF:(half + 1) * PEER_HALF]
            st = lax.dot_general(sk_ref[half], qh, (((1,), (1,)), ((), ())),
                                 preferred_element_type=F32)
            for c in range(n_slab):
                slab_ref[c * ROUTE_PITCH:c * ROUTE_PITCH + N_KEYS, :] = (
                    st[:, c * LANES:(c + 1) * LANES])
            for k in range(N_KEYS):
                km_ref[k] = slab_ref[pl.ds(k, n_slab, stride=ROUTE_PITCH), :]

            def extract(r, pos_prev):
                part = [None] * 8
                for k in range(N_KEYS):
                    v = jnp.where(pos_prev == float(k), -jnp.inf, km_ref[k])
                    km_ref[k] = v
                    part[k % 8] = v if part[k % 8] is None else jnp.maximum(part[k % 8], v)
                m = _tree(jnp.maximum, part)
                part = [None] * 8
                for k in range(N_KEYS):
                    c = jnp.where(km_ref[k] == m, float(k), big)
                    part[k % 8] = c if part[k % 8] is None else jnp.minimum(part[k % 8], c)
                pos = _tree(jnp.minimum, part)
                sv_ref[half, r] = m
                si_ref[half, r] = pos
                return pos

            lax.fori_loop(0, PEER_TOPK, extract, jnp.full((n_slab, LANES), -1.0, F32))

        sv0 = [sv_ref[0, a] for a in range(PEER_TOPK)]
        sv1 = [sv_ref[1, b] for b in range(PEER_TOPK)]
        cand = [sv0[a] + sv1[b] for a, b in cands]
        code = [float(a * PEER_TOPK + b) for a, b in cands]
        top_s, top_pos = [], []
        for _ in range(PEER_TOPK):
            m = _tree(jnp.maximum, cand)
            pos = _tree(jnp.minimum, [jnp.where(c == m, cd, big) for c, cd in zip(cand, code)])
            top_s.append(m)
            top_pos.append(pos)
            cand = [jnp.where(pos == cd, -jnp.inf, c) for c, cd in zip(cand, code)]
        si0 = [si_ref[0, a] for a in range(PEER_TOPK)]
        si1 = [si_ref[1, b] for b in range(PEER_TOPK)]
        e = [jnp.exp(t - top_s[0]) for t in top_s]
        denom = _tree(jnp.add, e)
        for r in range(PEER_TOPK):
            a_sel = jnp.floor(top_pos[r] * (1.0 / PEER_TOPK))
            b_sel = top_pos[r] - a_sel * PEER_TOPK
            i_sel = si0[0]
            j_sel = si1[0]
            for a in range(1, PEER_TOPK):
                i_sel = jnp.where(a_sel == float(a), si0[a], i_sel)
                j_sel = jnp.where(b_sel == float(a), si1[a], j_sel)
            row = hh * PEER_TOPK + r
            oi_ref[row] = i_sel
            oj_ref[row] = j_sel
            og_ref[row] = e[r] / denom
        return 0

    lax.fori_loop(0, PEER_HEADS, head, 0)

    for src, dst in ((oi_ref, isel_ref), (oj_ref, jsel_ref), (og_ref, g_ref)):
        for p in range(N_PAIRS):
            slab_ref[pl.ds(p, n_slab, stride=ROUTE_PITCH), :] = src[p]
        for c in range(n_slab):
            blk = slab_ref[c * ROUTE_PITCH:c * ROUTE_PITCH + N_PAIRS, :].T
            dst[c * LANES:(c + 1) * LANES, :] = blk.astype(dst.dtype)


def _peer_route(h2, wq, sk, tb):
    t, d = h2.shape
    tile = lambda i: (i, 0)
    one = pl.Buffered(1)
    n_slab = tb // LANES
    vregs = lambda n: pltpu.VMEM((n, n_slab, LANES), F32)
    return pl.pallas_call(
        functools.partial(_route_kernel, tb=tb),
        grid=(t // tb,),
        in_specs=[pl.BlockSpec((tb, d), tile),
                  pl.BlockSpec(wq.shape, lambda i: (0, 0, 0), pipeline_mode=one),
                  pl.BlockSpec(sk.shape, lambda i: (0, 0, 0), pipeline_mode=one)],
        out_specs=[pl.BlockSpec((tb, N_PAIRS), tile)] * 3,
        out_shape=[jax.ShapeDtypeStruct((t, N_PAIRS), I32),
                   jax.ShapeDtypeStruct((t, N_PAIRS), I32),
                   jax.ShapeDtypeStruct((t, N_PAIRS), F32)],
        scratch_shapes=[pltpu.VMEM((n_slab * ROUTE_PITCH, LANES), F32),
                        vregs(N_KEYS),
                        pltpu.VMEM((2, PEER_TOPK, n_slab, LANES), F32),
                        pltpu.VMEM((2, PEER_TOPK, n_slab, LANES), F32),
                        vregs(N_PAIRS), vregs(N_PAIRS), vregs(N_PAIRS)],
        compiler_params=pltpu.CompilerParams(
            dimension_semantics=("parallel",), vmem_limit_bytes=VMEM_LIMIT),
        name="peer_route",
    )(h2, wq, sk)


def _ffn_kernel(h2_ref, isel_ref, jsel_ref, g_ref, isel_nx_ref, jsel_nx_ref, g_nx_ref,
                dnt_ref, up_ref, out_ref, w_even_ref, w_odd_ref, *, tb, ec):
    i = pl.program_id(0)
    e = pl.program_id(1)
    n_chunk_mxu = ec // MXU_DIM
    per_step = tb // (N_KEYS * N_KEYS // ec)
    per_mxu_chunk = per_step // n_chunk_mxu
    key = lax.broadcasted_iota(I32, (N_KEYS, N_PAIRS), 0)

    def build_token(w_ref, ir, jr, gr, t):
        ii = jnp.broadcast_to(ir[pl.ds(t, 1), :], (N_KEYS, N_PAIRS))
        jj = jnp.broadcast_to(jr[pl.ds(t, 1), :], (N_KEYS, N_PAIRS))
        gg = jnp.broadcast_to(gr[pl.ds(t, 1), :], (N_KEYS, N_PAIRS))
        a = jnp.where(ii == key, 1.0, 0.0).astype(BF16)
        b = jnp.where(jj == key, gg, 0.0).astype(BF16)
        w_ref[pl.ds(pl.multiple_of(t * W_PITCH, 8), N_KEYS), :] = lax.dot_general(
            a, b, (((1,), (1,)), ((), ())), preferred_element_type=F32)

    def step(w_cur, w_nxt):
        @pl.when((i == 0) & (e == 0))
        def _first_tile():
            def tok(t, _):
                build_token(w_cur, isel_ref, jsel_ref, g_ref, t)
                return 0
            lax.fori_loop(0, tb, tok, 0, unroll=16)

        @pl.when(e == 0)
        def _zero():
            out_ref[...] = jnp.zeros_like(out_ref)

        h2 = h2_ref[...]
        tile0 = e * (ec // LANES)
        z = []
        for c in range(n_chunk_mxu):
            cols = slice(c * MXU_DIM, (c + 1) * MXU_DIM)
            pre = jnp.dot(h2, dnt_ref[:, cols], preferred_element_type=F32)
            for u in range(per_mxu_chunk):
                build_token(w_nxt, isel_nx_ref, jsel_nx_ref, g_nx_ref,
                            e * per_step + c * per_mxu_chunk + u)
            w = jnp.concatenate(
                [w_cur[pl.ds(tile0 + c * (MXU_DIM // LANES) + u, tb, stride=W_PITCH), :]
                 for u in range(MXU_DIM // LANES)], axis=-1)
            z.append((w * _gelu(pre)).astype(BF16))
        out_ref[...] += jnp.dot(jnp.concatenate(z, axis=-1), up_ref[...],
                                preferred_element_type=F32)

    @pl.when(i % 2 == 0)
    def _even():
        step(w_even_ref, w_odd_ref)

    @pl.when(i % 2 == 1)
    def _odd():
        step(w_odd_ref, w_even_ref)


def _peer_ffn(h2, isel, jsel, g, down_t, up, tb, ec):
    t, d = h2.shape
    n_exp = up.shape[0]
    n_tile = t // tb
    tile = lambda i, e: (i, 0)
    nxt = lambda i, e: (jnp.minimum(i + 1, n_tile - 1), 0)
    assert tb % (n_exp // ec) == 0 and (tb // (n_exp // ec)) % (ec // MXU_DIM) == 0
    return pl.pallas_call(
        functools.partial(_ffn_kernel, tb=tb, ec=ec),
        grid=(n_tile, n_exp // ec),
        in_specs=[pl.BlockSpec((tb, d), tile),
                  pl.BlockSpec((tb, N_PAIRS), tile),
                  pl.BlockSpec((tb, N_PAIRS), tile),
                  pl.BlockSpec((tb, N_PAIRS), tile),
                  pl.BlockSpec((tb, N_PAIRS), nxt),
                  pl.BlockSpec((tb, N_PAIRS), nxt),
                  pl.BlockSpec((tb, N_PAIRS), nxt),
                  pl.BlockSpec((d, ec), lambda i, e: (0, e)),
                  pl.BlockSpec((ec, d), lambda i, e: (e, 0))],
        out_specs=pl.BlockSpec((tb, d), tile),
        out_shape=jax.ShapeDtypeStruct((t, d), F32),
        scratch_shapes=[pltpu.VMEM((tb * W_PITCH, LANES), F32),
                        pltpu.VMEM((tb * W_PITCH, LANES), F32)],
        compiler_params=pltpu.CompilerParams(
            dimension_semantics=("arbitrary", "arbitrary"), vmem_limit_bytes=VMEM_LIMIT),
        name="peer_ffn",
    )(h2, isel, jsel, g, isel, jsel, g, down_t, up)


def _finish_kernel(x1_ref, y_ref, gate_ref, fg_ref, o_ref):
    x2 = x1_ref[0] + gate_ref[0] * y_ref[0]
    o_ref[0] = _rms(x2) * fg_ref[...]


def _finish(x1, y, gate_f, final_gain, tm):
    bsz, s, d = x1.shape
    tile = lambda b, i: (b, i, 0)
    return pl.pallas_call(
        _finish_kernel,
        grid=(bsz, s // tm),
        in_specs=[pl.BlockSpec((1, tm, d), tile),
                  pl.BlockSpec((1, tm, d), tile),
                  pl.BlockSpec((1, 1, d), lambda b, i: (b, 0, 0)),
                  pl.BlockSpec((1, d), lambda b, i: (0, 0))],
        out_specs=pl.BlockSpec((1, tm, d), tile),
        out_shape=jax.ShapeDtypeStruct((bsz, s, d), F32),
        compiler_params=pltpu.CompilerParams(
            dimension_semantics=("parallel", "parallel"), vmem_limit_bytes=VMEM_LIMIT),
        name="finish",
    )(x1, y, gate_f, final_gain)


def _layer(x, posf, invf, mod, l, p, tm, tq, tb_route, tb_ffn, ec):
    bsz, s, d = x.shape
    lam_init = 0.8 - 0.6 * math.exp(-0.3 * l)
    qt, k, vt, xp, gates = _mixer_in(
        x, posf, mod, p['norm_mix_gain'][l][None], invf, p['w_in'][l].astype(BF16),
        p['w_gate'][l].astype(BF16), p['b_gate'][l][None], tm)
    lam_vecs = jnp.stack([p['lambda_q1'][l], p['lambda_k1'][l],
                          p['lambda_q2'][l], p['lambda_k2'][l]])
    o = _diff_attn(lam_vecs, qt, k, vt, p['attn_subln_gain'][l][None], lam_init, tq, 4)
    x1, h2 = _merge_out(
        x, xp, o, gates, mod, p['norm_ffn_gain'][l][None], p['pool_w'][l].astype(BF16),
        p['pool_scale'][l][None], p['w_branch_pool'][l].astype(BF16),
        p['w_branch_attn'][l].astype(BF16), p['w_out'][l].astype(BF16), tm)
    h2f = h2.reshape(bsz * s, d)
    wq = p['peer_w_query'][l].astype(BF16).reshape(d, PEER_HEADS, 2 * PEER_HALF).transpose(1, 0, 2)
    isel, jsel, g = _peer_route(h2f, wq, p['peer_sub_keys'][l].astype(BF16), tb_route)
    return h2f, isel, jsel, g, x1


def kernel(x, c, positions, norm_mix_gain, norm_ffn_gain, w_ada, b_ada, w_in, w_gate, b_gate,
           pool_w, pool_scale, lambda_q1, lambda_k1, lambda_q2, lambda_k2, attn_subln_gain,
           w_branch_pool, w_branch_attn, w_out, peer_w_query, peer_sub_keys, peer_down, peer_up,
           final_norm_gain):
    bsz, s, d = x.shape
    depth = w_in.shape[0]
    assert depth == 1, "the driver below runs one layer and then the final RMSNorm"
    p = dict(norm_mix_gain=norm_mix_gain, norm_ffn_gain=norm_ffn_gain, w_in=w_in, w_gate=w_gate,
             b_gate=b_gate, pool_w=pool_w, pool_scale=pool_scale, lambda_q1=lambda_q1,
             lambda_k1=lambda_k1, lambda_q2=lambda_q2, lambda_k2=lambda_k2,
             attn_subln_gain=attn_subln_gain, w_branch_pool=w_branch_pool,
             w_branch_attn=w_branch_attn, w_out=w_out, peer_w_query=peer_w_query,
             peer_sub_keys=peer_sub_keys)
    tm = min(256, s)
    tq = min(256, s)
    tb_route = 1024
    tb_ffn = 256
    ec = 1024
    inv_freq = ROPE_THETA ** (-jnp.arange(0, ROT_DIM, 2, dtype=F32) / ROT_DIM)
    invf = jnp.tile(inv_freq, LANES // (ROT_DIM // 2))[None]
    posf = positions.astype(F32)[..., None]
    l = 0
    mod = _ada_mod(c, w_ada[l], b_ada[l]).reshape(bsz, N_MOD, d)
    h2f, isel, jsel, g, x1 = _layer(x, posf, invf, mod, l, p, tm, tq, tb_route, tb_ffn, ec)
    down_t = peer_down[l].astype(BF16).T
    y = _peer_ffn(h2f, isel, jsel, g, down_t, peer_up[l].astype(BF16), tb_ffn, ec)
    return _finish(x1, y.reshape(bsz, s, d), mod[:, 5:6, :], final_norm_gain[None], tm)
```

```python
import functools
import math

import jax
import jax.numpy as jnp
from jax import lax
from jax.experimental import pallas as pl
from jax.experimental.pallas import tpu as pltpu

F32 = jnp.float32
BF16 = jnp.bfloat16
I32 = jnp.int32

EPS = 1e-6
NEG_INF = -1e30

N_ATTN_HEADS = 8
ATTN_HEAD_DIM = 64
ATTN_V_DIM = 128
ROT_DIM = 16
ROPE_THETA = 500000.0
POOL_WINDOWS = (2, 4, 8, 16)
POOL_GROUP_DIM = 256
POOL_HALO = 16
N_MOD = 6

PEER_HEADS = 8
N_KEYS = 128
PEER_HALF = 128
PEER_TOPK = 16
N_PAIRS = PEER_HEADS * PEER_TOPK

LANES = 128
MXU_DIM = 256
VMEM_LIMIT = 56 * 1024 * 1024

W_PITCH = 136
ROUTE_PITCH = 136


def _rms(x, eps=EPS):
    return x * lax.rsqrt(jnp.mean(x * x, axis=-1, keepdims=True) + eps)


def _gelu(x):
    return 0.5 * x * (1.0 + lax.erf(x * math.sqrt(0.5)))


def _ada_kernel(c_ref, w_ref, b_ref, o_ref):
    ca = jax.nn.silu(c_ref[...])
    o_ref[...] = jnp.dot(ca, w_ref[...], preferred_element_type=F32,
                         precision=lax.Precision.HIGHEST) + b_ref[...]


def _ada_mod(c, w, b):
    bsz, d = c.shape
    n = w.shape[1]
    tn = 1536
    return pl.pallas_call(
        _ada_kernel,
        grid=(n // tn,),
        in_specs=[pl.BlockSpec((bsz, d), lambda j: (0, 0)),
                  pl.BlockSpec((d, tn), lambda j: (0, j)),
                  pl.BlockSpec((1, tn), lambda j: (0, j))],
        out_specs=pl.BlockSpec((bsz, tn), lambda j: (0, j)),
        out_shape=jax.ShapeDtypeStruct((bsz, n), F32),
        compiler_params=pltpu.CompilerParams(vmem_limit_bytes=VMEM_LIMIT),
        name="ada_mod",
    )(c, w, b.reshape(1, n))


def _mixer_in_kernel(x_ref, pos_ref, mod_ref, gain_ref, invf_ref, win_ref, wg_ref, bg_ref,
                     qt_ref, k_ref, vt_ref, xp_ref, gt_ref):
    d = x_ref.shape[-1]
    x = x_ref[0]
    shift = mod_ref[0, 0:1, :]
    scale = mod_ref[0, 1:2, :]
    h = (_rms(x) * gain_ref[...]) * (1.0 + scale) + shift
    hb = h.astype(BF16)

    ang = pos_ref[0] * invf_ref[...]
    cos = jnp.cos(ang)
    sin = jnp.sin(ang)
    lane = lax.broadcasted_iota(I32, (1, LANES), 1) % ATTN_HEAD_DIM
    half = ROT_DIM // 2
    cm = jnp.where(lane < ROT_DIM, cos, 1.0)
    sa = jnp.where(lane < half, -sin, 0.0)
    sb = jnp.where((lane >= half) & (lane < ROT_DIM), sin, 0.0)

    def rope(t):
        return (t * cm + pltpu.roll(t, LANES - half, 1) * sa + pltpu.roll(t, half, 1) * sb)

    p = jnp.dot(hb, win_ref[:, 0:d], preferred_element_type=F32)
    for hh in range(d // LANES):
        sl = slice(hh * LANES, (hh + 1) * LANES)
        qt_ref[0, sl, :] = (rope(p[:, sl]) * ATTN_HEAD_DIM ** -0.5).T.astype(qt_ref.dtype)
    p = jnp.dot(hb, win_ref[:, d:2 * d], preferred_element_type=F32)
    for hh in range(d // LANES):
        sl = slice(hh * LANES, (hh + 1) * LANES)
        k_ref[0, :, sl] = rope(p[:, sl]).astype(k_ref.dtype)
    p = jnp.dot(hb, win_ref[:, 2 * d:3 * d], preferred_element_type=F32)
    for hh in range(d // LANES):
        sl = slice(hh * LANES, (hh + 1) * LANES)
        vt_ref[0, sl, :] = p[:, sl].T.astype(vt_ref.dtype)
    xp_ref[0] = jnp.dot(hb, win_ref[:, 3 * d:4 * d], preferred_element_type=F32)
    for c in range(2):
        g = jnp.dot(hb, wg_ref[:, c * d:(c + 1) * d], preferred_element_type=F32)
        gt_ref[0, :, c * d:(c + 1) * d] = jax.nn.sigmoid(
            g + bg_ref[:, c * d:(c + 1) * d]).astype(gt_ref.dtype)


def _mixer_in(x, posf, mod, gain, invf, w_in, w_gate, b_gate, tm):
    bsz, s, d = x.shape
    const = lambda b, i: (0, 0)
    tile = lambda b, i: (b, i, 0)
    ttile = lambda b, i: (b, 0, i)
    one = pl.Buffered(1)
    return pl.pallas_call(
        _mixer_in_kernel,
        grid=(bsz, s // tm),
        in_specs=[pl.BlockSpec((1, tm, d), tile),
                  pl.BlockSpec((1, tm, 1), tile),
                  pl.BlockSpec((1, N_MOD, d), lambda b, i: (b, 0, 0)),
                  pl.BlockSpec((1, d), const),
                  pl.BlockSpec((1, LANES), const),
                  pl.BlockSpec(w_in.shape, const, pipeline_mode=one),
                  pl.BlockSpec(w_gate.shape, const, pipeline_mode=one),
                  pl.BlockSpec((1, 2 * d), const)],
        out_specs=[pl.BlockSpec((1, d, tm), ttile), pl.BlockSpec((1, tm, d), tile),
                   pl.BlockSpec((1, d, tm), ttile), pl.BlockSpec((1, tm, d), tile),
                   pl.BlockSpec((1, tm, 2 * d), tile)],
        out_shape=[jax.ShapeDtypeStruct((bsz, d, s), BF16),
                   jax.ShapeDtypeStruct((bsz, s, d), BF16),
                   jax.ShapeDtypeStruct((bsz, d, s), BF16),
                   jax.ShapeDtypeStruct((bsz, s, d), F32),
                   jax.ShapeDtypeStruct((bsz, s, 2 * d), BF16)],
        compiler_params=pltpu.CompilerParams(
            dimension_semantics=("parallel", "parallel"), vmem_limit_bytes=VMEM_LIMIT),
        name="mixer_in",
    )(x, posf, mod, gain, invf, w_in, w_gate, b_gate)


def _attn_kernel(lam_ref, qt_ref, k_ref, vt_ref, gain_ref, o_ref, *, nblk, tq, hps, lam_init):
    sub = lax.broadcasted_iota(I32, (LANES, tq), 0)
    zero = jnp.zeros((LANES, tq), BF16)
    qq = []
    for h in range(hps):
        qt = qt_ref[0, h * LANES:(h + 1) * LANES, :]
        qq.append(jnp.concatenate([jnp.where(sub < ATTN_HEAD_DIM, qt, zero),
                                   jnp.where(sub >= ATTN_HEAD_DIM, qt, zero)], axis=1))
    m = [jnp.full((1, 2 * tq), NEG_INF, F32)] * hps
    l = [jnp.zeros((1, 2 * tq), F32)] * hps
    acc = [jnp.zeros((ATTN_V_DIM, 2 * tq), F32)] * hps

    def scores(j, h):
        return jnp.dot(k_ref[0, j * tq:(j + 1) * tq, h * LANES:(h + 1) * LANES], qq[h],
                       preferred_element_type=F32)

    s_next = [scores(0, h) for h in range(hps)]
    for j in range(nblk):
        for h in range(hps):
            s = s_next[h]
            if j + 1 < nblk:
                s_next[h] = scores(j + 1, h)
            if j == nblk - 1:
                kv = lax.broadcasted_iota(I32, s.shape, 0)
                qr = lax.broadcasted_iota(I32, s.shape, 1) % tq
                s = jnp.where(qr >= kv, s, NEG_INF)
            m_new = jnp.maximum(m[h], jnp.max(s, axis=0, keepdims=True))
            alpha = jnp.exp(m[h] - m_new)
            p = jnp.exp(s - m_new)
            l[h] = alpha * l[h] + jnp.sum(p, axis=0, keepdims=True)
            vt = vt_ref[0, h * LANES:(h + 1) * LANES, j * tq:(j + 1) * tq]
            acc[h] = alpha * acc[h] + jnp.dot(vt, p.astype(BF16),
                                              preferred_element_type=F32)
            m[h] = m_new

    lv = lam_ref[...]
    lam = (jnp.exp(jnp.sum(lv[0:1] * lv[1:2], axis=-1, keepdims=True))
           - jnp.exp(jnp.sum(lv[2:3] * lv[3:4], axis=-1, keepdims=True)) + lam_init)
    for h in range(hps):
        cols = slice(h * LANES, (h + 1) * LANES)
        ot = acc[h] * (1.0 / l[h])
        o = (ot[:, :tq] - lam * ot[:, tq:]).T
        o_ref[0, :, cols] = (_rms(o) * gain_ref[:, cols] * (1.0 - lam_init)).astype(o_ref.dtype)


def _diff_attn(lam_vecs, qt, k, vt, gain, lam_init, tq, hps):
    bsz, s, d = k.shape
    w = hps * LANES
    outs = []
    for qi in range(s // tq):
        kv_len = (qi + 1) * tq
        outs.append(pl.pallas_call(
            functools.partial(_attn_kernel, nblk=qi + 1, tq=tq, hps=hps, lam_init=lam_init),
            grid=(bsz, d // w),
            in_specs=[pl.BlockSpec(lam_vecs.shape, lambda b, h: (0, 0)),
                      pl.BlockSpec((1, w, tq), lambda b, h, qi=qi: (b, h, qi)),
                      pl.BlockSpec((1, kv_len, w), lambda b, h: (b, 0, h)),
                      pl.BlockSpec((1, w, kv_len), lambda b, h: (b, h, 0)),
                      pl.BlockSpec((1, w), lambda b, h: (0, h))],
            out_specs=pl.BlockSpec((1, tq, w), lambda b, h: (b, 0, h)),
            out_shape=jax.ShapeDtypeStruct((bsz, tq, d), BF16),
            compiler_params=pltpu.CompilerParams(
                dimension_semantics=("parallel", "parallel"), vmem_limit_bytes=VMEM_LIMIT),
            name=f"diff_attn_q{qi}",
        )(lam_vecs, qt, k, vt, gain))
    return jnp.concatenate(outs, axis=1)


def _merge_kernel(x_ref, xp_ref, halo_ref, o_ref, gt_ref, mod_ref, gain_ref, pw_ref, ps_ref,
                  wbp_ref, wba_ref, wout_ref, x1_ref, h2_ref, xs_ref, *, tm):
    i = pl.program_id(1)
    d = x_ref.shape[-1]
    halo = halo_ref[0]
    xs_ref[0:POOL_HALO, :] = jnp.where(i > 0, halo, jnp.zeros_like(halo))
    xs_ref[POOL_HALO:, :] = xp_ref[0]
    pos = i * tm + lax.broadcasted_iota(I32, (tm, 1), 0)
    y_pool = None
    ya = []
    for g, w in enumerate(POOL_WINDOWS):
        cols = slice(g * POOL_GROUP_DIM, (g + 1) * POOL_GROUP_DIM)
        tok = xs_ref[POOL_HALO:, cols]
        acc = tok
        for u in range(1, w):
            acc = acc + xs_ref[POOL_HALO - u:POOL_HALO - u + tm, cols]
        cnt = jnp.minimum(pos + 1, w).astype(F32)
        pooled = acc / cnt - tok
        yg = jnp.dot(pooled.astype(BF16), pw_ref[g], preferred_element_type=F32)
        ya.append((yg * ps_ref[:, cols]).astype(BF16))
    ya = jnp.concatenate(ya, axis=-1)
    y_pool = jnp.dot(ya, wbp_ref[...], preferred_element_type=F32)
    y_attn = jnp.dot(o_ref[0], wba_ref[...], preferred_element_type=F32)
    merged = (gt_ref[0, :, 0:d].astype(F32) * y_pool + gt_ref[0, :, d:2 * d].astype(F32) * y_attn)
    mix = jnp.dot(merged.astype(BF16), wout_ref[...], preferred_element_type=F32)
    x1 = x_ref[0] + mod_ref[0, 2:3, :] * mix
    x1_ref[0] = x1
    h2 = (_rms(x1) * gain_ref[...]) * (1.0 + mod_ref[0, 4:5, :]) + mod_ref[0, 3:4, :]
    h2_ref[0] = h2.astype(h2_ref.dtype)


def _merge_out(x, xp, o, gates, mod, gain_f, pool_w, pool_scale, wbp, wba, wout, tm):
    bsz, s, d = x.shape
    const2 = lambda b, i: (0, 0)
    tile = lambda b, i: (b, i, 0)
    hb = tm // POOL_HALO
    one = pl.Buffered(1)
    return pl.pallas_call(
        functools.partial(_merge_kernel, tm=tm),
        grid=(bsz, s // tm),
        in_specs=[pl.BlockSpec((1, tm, d), tile),
                  pl.BlockSpec((1, tm, d), tile),
                  pl.BlockSpec((1, POOL_HALO, d), lambda b, i: (b, jnp.maximum(i * hb - 1, 0), 0)),
                  pl.BlockSpec((1, tm, d), tile),
                  pl.BlockSpec((1, tm, 2 * d), tile),
                  pl.BlockSpec((1, N_MOD, d), lambda b, i: (b, 0, 0)),
                  pl.BlockSpec((1, d), const2),
                  pl.BlockSpec(pool_w.shape, lambda b, i: (0, 0, 0), pipeline_mode=one),
                  pl.BlockSpec((1, d), const2),
                  pl.BlockSpec(wbp.shape, const2, pipeline_mode=one),
                  pl.BlockSpec(wba.shape, const2, pipeline_mode=one),
                  pl.BlockSpec(wout.shape, const2, pipeline_mode=one)],
        out_specs=[pl.BlockSpec((1, tm, d), tile), pl.BlockSpec((1, tm, d), tile)],
        out_shape=[jax.ShapeDtypeStruct((bsz, s, d), F32),
                   jax.ShapeDtypeStruct((bsz, s, d), BF16)],
        scratch_shapes=[pltpu.VMEM((tm + POOL_HALO, d), F32)],
        compiler_params=pltpu.CompilerParams(
            dimension_semantics=("parallel", "parallel"), vmem_limit_bytes=VMEM_LIMIT),
        name="merge_out",
    )(x, xp, xp, o, gates, mod, gain_f, pool_w, pool_scale, wbp, wba, wout)


def _staircase():
    return [(a, b) for a in range(PEER_TOPK) for b in range(PEER_TOPK)
            if (a + 1) * (b + 1) <= PEER_TOPK]


def _tree(op, xs):
    xs = list(xs)
    while len(xs) > 1:
        xs = [op(xs[i], xs[i + 1]) if i + 1 < len(xs) else xs[i] for i in range(0, len(xs), 2)]
    return xs[0]


def _route_kernel(h2_ref, wq_ref, sk_ref, isel_ref, jsel_ref, g_ref,
                  slab_ref, km_ref, sv_ref, si_ref, oi_ref, oj_ref, og_ref, *, tb):
    n_slab = tb // LANES
    big = 1e9
    h2 = h2_ref[...]
    cands = _staircase()

    def head(hh, _):
        q = jnp.dot(h2, wq_ref[hh], preferred_element_type=F32).astype(BF16)
        for half in range(2):
            qh = q[:, half * PEER_HALF:(half + 1) * PEER_HALF]
            st = lax.dot_general(sk_ref[half], qh, (((1,), (1,)), ((), ())),
                                 preferred_element_type=F32)
            for c in range(n_slab):
                slab_ref[c * ROUTE_PITCH:c * ROUTE_PITCH + N_KEYS, :] = (
                    st[:, c * LANES:(c + 1) * LANES])
            for k in range(N_KEYS):
                km_ref[k] = slab_ref[pl.ds(k, n_slab, stride=ROUTE_PITCH), :]

            def extract(r, pos_prev):
                part = [None] * 8
                for k in range(N_KEYS):
                    v = jnp.where(pos_prev == float(k), -jnp.inf, km_ref[k])
                    km_ref[k] = v
                    if part[k % 8] is None:
                        part[k % 8] = (v, jnp.full_like(v, float(k)))
                    else:
                        bm, bk = part[k % 8]
                        part[k % 8] = (jnp.maximum(bm, v), jnp.where(v > bm, float(k), bk))

                def lower_key_wins(x, y):
                    take_y = (y[0] > x[0]) | ((y[0] == x[0]) & (y[1] < x[1]))
                    return jnp.where(take_y, y[0], x[0]), jnp.where(take_y, y[1], x[1])

                m, pos = _tree(lower_key_wins, part)
                sv_ref[half, r] = m
                si_ref[half, r] = pos
                return pos

            lax.fori_loop(0, PEER_TOPK, extract, jnp.full((n_slab, LANES), -1.0, F32))

        sv0 = [sv_ref[0, a] for a in range(PEER_TOPK)]
        sv1 = [sv_ref[1, b] for b in range(PEER_TOPK)]
        cand = [sv0[a] + sv1[b] for a, b in cands]
        code = [float(a * PEER_TOPK + b) for a, b in cands]
        top_s, top_pos = [], []
        for _ in range(PEER_TOPK):
            m = _tree(jnp.maximum, cand)
            pos = _tree(jnp.minimum, [jnp.where(c == m, cd, big) for c, cd in zip(cand, code)])
            top_s.append(m)
            top_pos.append(pos)
            cand = [jnp.where(pos == cd, -jnp.inf, c) for c, cd in zip(cand, code)]
        si0 = [si_ref[0, a] for a in range(PEER_TOPK)]
        si1 = [si_ref[1, b] for b in range(PEER_TOPK)]
        e = [jnp.exp(t - top_s[0]) for t in top_s]
        denom = _tree(jnp.add, e)
        for r in range(PEER_TOPK):
            a_sel = jnp.floor(top_pos[r] * (1.0 / PEER_TOPK))
            b_sel = top_pos[r] - a_sel * PEER_TOPK
            i_sel = si0[0]
            j_sel = si1[0]
            for a in range(1, PEER_TOPK):
                i_sel = jnp.where(a_sel == float(a), si0[a], i_sel)
                j_sel = jnp.where(b_sel == float(a), si1[a], j_sel)
            row = hh * PEER_TOPK + r
            oi_ref[row] = i_sel
            oj_ref[row] = j_sel
            og_ref[row] = e[r] / denom
        return 0

    lax.fori_loop(0, PEER_HEADS, head, 0)

    for src, dst in ((oi_ref, isel_ref), (oj_ref, jsel_ref), (og_ref, g_ref)):
        for p in range(N_PAIRS):
            slab_ref[pl.ds(p, n_slab, stride=ROUTE_PITCH), :] = src[p]
        for c in range(n_slab):
            blk = slab_ref[c * ROUTE_PITCH:c * ROUTE_PITCH + N_PAIRS, :].T
            dst[c * LANES:(c + 1) * LANES, :] = blk.astype(dst.dtype)


def _peer_route(h2, wq, sk, tb):
    t, d = h2.shape
    tile = lambda i: (i, 0)
    one = pl.Buffered(1)
    n_slab = tb // LANES
    vregs = lambda n: pltpu.VMEM((n, n_slab, LANES), F32)
    return pl.pallas_call(
        functools.partial(_route_kernel, tb=tb),
        grid=(t // tb,),
        in_specs=[pl.BlockSpec((tb, d), tile),
                  pl.BlockSpec(wq.shape, lambda i: (0, 0, 0), pipeline_mode=one),
                  pl.BlockSpec(sk.shape, lambda i: (0, 0, 0), pipeline_mode=one)],
        out_specs=[pl.BlockSpec((tb, N_PAIRS), tile)] * 3,
        out_shape=[jax.ShapeDtypeStruct((t, N_PAIRS), I32),
                   jax.ShapeDtypeStruct((t, N_PAIRS), I32),
                   jax.ShapeDtypeStruct((t, N_PAIRS), F32)],
        scratch_shapes=[pltpu.VMEM((n_slab * ROUTE_PITCH, LANES), F32),
                        vregs(N_KEYS),
                        pltpu.VMEM((2, PEER_TOPK, n_slab, LANES), F32),
                        pltpu.VMEM((2, PEER_TOPK, n_slab, LANES), F32),
                        vregs(N_PAIRS), vregs(N_PAIRS), vregs(N_PAIRS)],
        compiler_params=pltpu.CompilerParams(
            dimension_semantics=("parallel",), vmem_limit_bytes=VMEM_LIMIT),
        name="peer_route",
    )(h2, wq, sk)


def _ffn_kernel(h2_ref, isel_ref, jsel_ref, g_ref, dnt_ref, up_ref, out_ref, w_ref, *, tb, ec):
    e = pl.program_id(1)

    @pl.when(e == 0)
    def _build():
        out_ref[...] = jnp.zeros_like(out_ref)
        key = lax.broadcasted_iota(I32, (N_KEYS, N_PAIRS), 0)

        def tok(t, _):
            ii = jnp.broadcast_to(isel_ref[pl.ds(t, 1), :], (N_KEYS, N_PAIRS))
            jj = jnp.broadcast_to(jsel_ref[pl.ds(t, 1), :], (N_KEYS, N_PAIRS))
            gg = jnp.broadcast_to(g_ref[pl.ds(t, 1), :], (N_KEYS, N_PAIRS))
            a = jnp.where(ii == key, 1.0, 0.0).astype(BF16)
            b = jnp.where(jj == key, gg, 0.0).astype(BF16)
            w_ref[pl.ds(pl.multiple_of(t * W_PITCH, 8), N_KEYS), :] = lax.dot_general(
                a, b, (((1,), (1,)), ((), ())), preferred_element_type=F32)
            return 0

        lax.fori_loop(0, tb, tok, 0, unroll=16)

    h2 = h2_ref[...]
    tile0 = e * (ec // LANES)
    z = []
    for c in range(ec // MXU_DIM):
        cols = slice(c * MXU_DIM, (c + 1) * MXU_DIM)
        pre = jnp.dot(h2, dnt_ref[:, cols], preferred_element_type=F32)
        w = jnp.concatenate(
            [w_ref[pl.ds(tile0 + c * (MXU_DIM // LANES) + u, tb, stride=W_PITCH), :]
             for u in range(MXU_DIM // LANES)], axis=-1)
        z.append((w * _gelu(pre)).astype(BF16))
    out_ref[...] += jnp.dot(jnp.concatenate(z, axis=-1), up_ref[...],
                            preferred_element_type=F32)


def _peer_ffn(h2, isel, jsel, g, down_t, up, tb, ec):
    t, d = h2.shape
    n_exp = up.shape[0]
    tile = lambda i, e: (i, 0)
    return pl.pallas_call(
        functools.partial(_ffn_kernel, tb=tb, ec=ec),
        grid=(t // tb, n_exp // ec),
        in_specs=[pl.BlockSpec((tb, d), tile),
                  pl.BlockSpec((tb, N_PAIRS), tile),
                  pl.BlockSpec((tb, N_PAIRS), tile),
                  pl.BlockSpec((tb, N_PAIRS), tile),
                  pl.BlockSpec((d, ec), lambda i, e: (0, e)),
                  pl.BlockSpec((ec, d), lambda i, e: (e, 0))],
        out_specs=pl.BlockSpec((tb, d), tile),
        out_shape=jax.ShapeDtypeStruct((t, d), F32),
        scratch_shapes=[pltpu.VMEM((tb * W_PITCH, LANES), F32)],
        compiler_params=pltpu.CompilerParams(
            dimension_semantics=("parallel", "arbitrary"), vmem_limit_bytes=VMEM_LIMIT),
        name="peer_ffn",
    )(h2, isel, jsel, g, down_t, up)


def _finish_kernel(x1_ref, y_ref, gate_ref, fg_ref, o_ref):
    x2 = x1_ref[0] + gate_ref[0] * y_ref[0]
    o_ref[0] = _rms(x2) * fg_ref[...]


def _finish(x1, y, gate_f, final_gain, tm):
    bsz, s, d = x1.shape
    tile = lambda b, i: (b, i, 0)
    return pl.pallas_call(
        _finish_kernel,
        grid=(bsz, s // tm),
        in_specs=[pl.BlockSpec((1, tm, d), tile),
                  pl.BlockSpec((1, tm, d), tile),
                  pl.BlockSpec((1, 1, d), lambda b, i: (b, 0, 0)),
                  pl.BlockSpec((1, d), lambda b, i: (0, 0))],
        out_specs=pl.BlockSpec((1, tm, d), tile),
        out_shape=jax.ShapeDtypeStruct((bsz, s, d), F32),
        compiler_params=pltpu.CompilerParams(
            dimension_semantics=("parallel", "parallel"), vmem_limit_bytes=VMEM_LIMIT),
        name="finish",
    )(x1, y, gate_f, final_gain)


def _layer(x, posf, invf, mod, l, p, tm, tq, tb_route, tb_ffn, ec):
    bsz, s, d = x.shape
    lam_init = 0.8 - 0.6 * math.exp(-0.3 * l)
    qt, k, vt, xp, gates = _mixer_in(
        x, posf, mod, p['norm_mix_gain'][l][None], invf, p['w_in'][l].astype(BF16),
        p['w_gate'][l].astype(BF16), p['b_gate'][l][None], tm)
    lam_vecs = jnp.stack([p['lambda_q1'][l], p['lambda_k1'][l],
                          p['lambda_q2'][l], p['lambda_k2'][l]])
    o = _diff_attn(lam_vecs, qt, k, vt, p['attn_subln_gain'][l][None], lam_init, tq, 4)
    x1, h2 = _merge_out(
        x, xp, o, gates, mod, p['norm_ffn_gain'][l][None], p['pool_w'][l].astype(BF16),
        p['pool_scale'][l][None], p['w_branch_pool'][l].astype(BF16),
        p['w_branch_attn'][l].astype(BF16), p['w_out'][l].astype(BF16), tm)
    h2f = h2.reshape(bsz * s, d)
    wq = p['peer_w_query'][l].astype(BF16).reshape(d, PEER_HEADS, 2 * PEER_HALF).transpose(1, 0, 2)
    isel, jsel, g = _peer_route(h2f, wq, p['peer_sub_keys'][l].astype(BF16), tb_route)
    return h2f, isel, jsel, g, x1


def kernel(x, c, positions, norm_mix_gain, norm_ffn_gain, w_ada, b_ada, w_in, w_gate, b_gate,
           pool_w, pool_scale, lambda_q1, lambda_k1, lambda_q2, lambda_k2, attn_subln_gain,
           w_branch_pool, w_branch_attn, w_out, peer_w_query, peer_sub_keys, peer_down, peer_up,
           final_norm_gain):
    bsz, s, d = x.shape
    depth = w_in.shape[0]
    assert depth == 1, "the driver below runs one layer and then the final RMSNorm"
    p = dict(norm_mix_gain=norm_mix_gain, norm_ffn_gain=norm_ffn_gain, w_in=w_in, w_gate=w_gate,
             b_gate=b_gate, pool_w=pool_w, pool_scale=pool_scale, lambda_q1=lambda_q1,
             lambda_k1=lambda_k1, lambda_q2=lambda_q2, lambda_k2=lambda_k2,
             attn_subln_gain=attn_subln_gain, w_branch_pool=w_branch_pool,
             w_branch_attn=w_branch_attn, w_out=w_out, peer_w_query=peer_w_query,
             peer_sub_keys=peer_sub_keys)
    tm = min(256, s)
    tq = min(256, s)
    tb_route = 1024
    tb_ffn = 512
    ec = 1024
    inv_freq = ROPE_THETA ** (-jnp.arange(0, ROT_DIM, 2, dtype=F32) / ROT_DIM)
    invf = jnp.tile(inv_freq, LANES // (ROT_DIM // 2))[None]
    posf = positions.astype(F32)[..., None]
    l = 0
    mod = _ada_mod(c, w_ada[l], b_ada[l]).reshape(bsz, N_MOD, d)
    h2f, isel, jsel, g, x1 = _layer(x, posf, invf, mod, l, p, tm, tq, tb_route, tb_ffn, ec)
    down_t = peer_down[l].astype(BF16).T
    y = _peer_ffn(h2f, isel, jsel, g, down_t, peer_up[l].astype(BF16), tb_ffn, ec)
    return _finish(x1, y.reshape(bsz, s, d), mod[:, 5:6, :], final_norm_gain[None], tm)
```

```python
import functools
import math

import jax
import jax.numpy as jnp
from jax import lax
from jax.experimental import pallas as pl
from jax.experimental.pallas import tpu as pltpu

F32 = jnp.float32
BF16 = jnp.bfloat16
I32 = jnp.int32

EPS = 1e-6
NEG_INF = -1e30

N_ATTN_HEADS = 8
ATTN_HEAD_DIM = 64
ATTN_V_DIM = 128
ROT_DIM = 16
ROPE_THETA = 500000.0
POOL_WINDOWS = (2, 4, 8, 16)
POOL_GROUP_DIM = 256
POOL_HALO = 16
N_MOD = 6

PEER_HEADS = 8
N_KEYS = 128
PEER_HALF = 128
PEER_TOPK = 16
N_PAIRS = PEER_HEADS * PEER_TOPK

LANES = 128
MXU_DIM = 256
VMEM_LIMIT = 56 * 1024 * 1024

W_PITCH = 136
ROUTE_PITCH = 136


def _rms(x, eps=EPS):
    return x * lax.rsqrt(jnp.mean(x * x, axis=-1, keepdims=True) + eps)


def _gelu(x):
    return 0.5 * x * (1.0 + lax.erf(x * math.sqrt(0.5)))


def _ada_kernel(c_ref, w_ref, b_ref, o_ref):
    ca = jax.nn.silu(c_ref[...])
    o_ref[...] = jnp.dot(ca, w_ref[...], preferred_element_type=F32,
                         precision=lax.Precision.HIGHEST) + b_ref[...]


def _ada_mod(c, w, b):
    bsz, d = c.shape
    n = w.shape[1]
    tn = 1536
    return pl.pallas_call(
        _ada_kernel,
        grid=(n // tn,),
        in_specs=[pl.BlockSpec((bsz, d), lambda j: (0, 0)),
                  pl.BlockSpec((d, tn), lambda j: (0, j)),
                  pl.BlockSpec((1, tn), lambda j: (0, j))],
        out_specs=pl.BlockSpec((bsz, tn), lambda j: (0, j)),
        out_shape=jax.ShapeDtypeStruct((bsz, n), F32),
        compiler_params=pltpu.CompilerParams(vmem_limit_bytes=VMEM_LIMIT),
        name="ada_mod",
    )(c, w, b.reshape(1, n))


def _mixer_in_kernel(x_ref, pos_ref, mod_ref, gain_ref, invf_ref, win_ref, wg_ref, bg_ref,
                     qt_ref, k_ref, vt_ref, xp_ref, gt_ref):
    d = x_ref.shape[-1]
    x = x_ref[0]
    shift = mod_ref[0, 0:1, :]
    scale = mod_ref[0, 1:2, :]
    h = (_rms(x) * gain_ref[...]) * (1.0 + scale) + shift
    hb = h.astype(BF16)

    ang = pos_ref[0] * invf_ref[...]
    cos = jnp.cos(ang)
    sin = jnp.sin(ang)
    lane = lax.broadcasted_iota(I32, (1, LANES), 1) % ATTN_HEAD_DIM
    half = ROT_DIM // 2
    cm = jnp.where(lane < ROT_DIM, cos, 1.0)
    sa = jnp.where(lane < half, -sin, 0.0)
    sb = jnp.where((lane >= half) & (lane < ROT_DIM), sin, 0.0)

    def rope(t):
        return (t * cm + pltpu.roll(t, LANES - half, 1) * sa + pltpu.roll(t, half, 1) * sb)

    p = jnp.dot(hb, win_ref[:, 0:d], preferred_element_type=F32)
    for hh in range(d // LANES):
        sl = slice(hh * LANES, (hh + 1) * LANES)
        qt_ref[0, sl, :] = (rope(p[:, sl]) * ATTN_HEAD_DIM ** -0.5).T.astype(qt_ref.dtype)
    p = jnp.dot(hb, win_ref[:, d:2 * d], preferred_element_type=F32)
    for hh in range(d // LANES):
        sl = slice(hh * LANES, (hh + 1) * LANES)
        k_ref[0, :, sl] = rope(p[:, sl]).astype(k_ref.dtype)
    p = jnp.dot(hb, win_ref[:, 2 * d:3 * d], preferred_element_type=F32)
    for hh in range(d // LANES):
        sl = slice(hh * LANES, (hh + 1) * LANES)
        vt_ref[0, sl, :] = p[:, sl].T.astype(vt_ref.dtype)
    xp_ref[0] = jnp.dot(hb, win_ref[:, 3 * d:4 * d], preferred_element_type=F32)
    for c in range(2):
        g = jnp.dot(hb, wg_ref[:, c * d:(c + 1) * d], preferred_element_type=F32)
        gt_ref[0, :, c * d:(c + 1) * d] = jax.nn.sigmoid(
            g + bg_ref[:, c * d:(c + 1) * d]).astype(gt_ref.dtype)


def _mixer_in(x, posf, mod, gain, invf, w_in, w_gate, b_gate, tm):
    bsz, s, d = x.shape
    const = lambda b, i: (0, 0)
    tile = lambda b, i: (b, i, 0)
    ttile = lambda b, i: (b, 0, i)
    one = pl.Buffered(1)
    return pl.pallas_call(
        _mixer_in_kernel,
        grid=(bsz, s // tm),
        in_specs=[pl.BlockSpec((1, tm, d), tile),
                  pl.BlockSpec((1, tm, 1), tile),
                  pl.BlockSpec((1, N_MOD, d), lambda b, i: (b, 0, 0)),
                  pl.BlockSpec((1, d), const),
                  pl.BlockSpec((1, LANES), const),
                  pl.BlockSpec(w_in.shape, const, pipeline_mode=one),
                  pl.BlockSpec(w_gate.shape, const, pipeline_mode=one),
                  pl.BlockSpec((1, 2 * d), const)],
        out_specs=[pl.BlockSpec((1, d, tm), ttile), pl.BlockSpec((1, tm, d), tile),
                   pl.BlockSpec((1, d, tm), ttile), pl.BlockSpec((1, tm, d), tile),
                   pl.BlockSpec((1, tm, 2 * d), tile)],
        out_shape=[jax.ShapeDtypeStruct((bsz, d, s), BF16),
                   jax.ShapeDtypeStruct((bsz, s, d), BF16),
                   jax.ShapeDtypeStruct((bsz, d, s), BF16),
                   jax.ShapeDtypeStruct((bsz, s, d), F32),
                   jax.ShapeDtypeStruct((bsz, s, 2 * d), BF16)],
        compiler_params=pltpu.CompilerParams(
            dimension_semantics=("parallel", "parallel"), vmem_limit_bytes=VMEM_LIMIT),
        name="mixer_in",
    )(x, posf, mod, gain, invf, w_in, w_gate, b_gate)


def _attn_kernel(lam_ref, qt_ref, k_ref, vt_ref, gain_ref, o_ref, *, nblk, tq, hps, lam_init):
    sub = lax.broadcasted_iota(I32, (LANES, tq), 0)
    zero = jnp.zeros((LANES, tq), BF16)
    qq = []
    for h in range(hps):
        qt = qt_ref[0, h * LANES:(h + 1) * LANES, :]
        qq.append(jnp.concatenate([jnp.where(sub < ATTN_HEAD_DIM, qt, zero),
                                   jnp.where(sub >= ATTN_HEAD_DIM, qt, zero)], axis=1))
    m = [jnp.full((1, 2 * tq), NEG_INF, F32)] * hps
    l = [jnp.zeros((1, 2 * tq), F32)] * hps
    acc = [jnp.zeros((ATTN_V_DIM, 2 * tq), F32)] * hps

    def scores(j, h):
        return jnp.dot(k_ref[0, j * tq:(j + 1) * tq, h * LANES:(h + 1) * LANES], qq[h],
                       preferred_element_type=F32)

    s_next = [scores(0, h) for h in range(hps)]
    for j in range(nblk):
        for h in range(hps):
            s = s_next[h]
            if j + 1 < nblk:
                s_next[h] = scores(j + 1, h)
            if j == nblk - 1:
                kv = lax.broadcasted_iota(I32, s.shape, 0)
                qr = lax.broadcasted_iota(I32, s.shape, 1) % tq
                s = jnp.where(qr >= kv, s, NEG_INF)
            m_new = jnp.maximum(m[h], jnp.max(s, axis=0, keepdims=True))
            alpha = jnp.exp(m[h] - m_new)
            p = jnp.exp(s - m_new)
            l[h] = alpha * l[h] + jnp.sum(p, axis=0, keepdims=True)
            vt = vt_ref[0, h * LANES:(h + 1) * LANES, j * tq:(j + 1) * tq]
            acc[h] = alpha * acc[h] + jnp.dot(vt, p.astype(BF16),
                                              preferred_element_type=F32)
            m[h] = m_new

    lv = lam_ref[...]
    lam = (jnp.exp(jnp.sum(lv[0:1] * lv[1:2], axis=-1, keepdims=True))
           - jnp.exp(jnp.sum(lv[2:3] * lv[3:4], axis=-1, keepdims=True)) + lam_init)
    for h in range(hps):
        cols = slice(h * LANES, (h + 1) * LANES)
        ot = acc[h] * (1.0 / l[h])
        o = (ot[:, :tq] - lam * ot[:, tq:]).T
        o_ref[0, :, cols] = (_rms(o) * gain_ref[:, cols] * (1.0 - lam_init)).astype(o_ref.dtype)


def _diff_attn(lam_vecs, qt, k, vt, gain, lam_init, tq, hps):
    bsz, s, d = k.shape
    w = hps * LANES
    outs = []
    for qi in range(s // tq):
        kv_len = (qi + 1) * tq
        outs.append(pl.pallas_call(
            functools.partial(_attn_kernel, nblk=qi + 1, tq=tq, hps=hps, lam_init=lam_init),
            grid=(bsz, d // w),
            in_specs=[pl.BlockSpec(lam_vecs.shape, lambda b, h: (0, 0)),
                      pl.BlockSpec((1, w, tq), lambda b, h, qi=qi: (b, h, qi)),
                      pl.BlockSpec((1, kv_len, w), lambda b, h: (b, 0, h)),
                      pl.BlockSpec((1, w, kv_len), lambda b, h: (b, h, 0)),
                      pl.BlockSpec((1, w), lambda b, h: (0, h))],
            out_specs=pl.BlockSpec((1, tq, w), lambda b, h: (b, 0, h)),
            out_shape=jax.ShapeDtypeStruct((bsz, tq, d), BF16),
            compiler_params=pltpu.CompilerParams(
                dimension_semantics=("parallel", "parallel"), vmem_limit_bytes=VMEM_LIMIT),
            name=f"diff_attn_q{qi}",
        )(lam_vecs, qt, k, vt, gain))
    return jnp.concatenate(outs, axis=1)


def _merge_kernel(x_ref, xp_ref, halo_ref, o_ref, gt_ref, mod_ref, gain_ref, pw_ref, ps_ref,
                  wbp_ref, wba_ref, wout_ref, x1_ref, h2_ref, xs_ref, *, tm):
    i = pl.program_id(1)
    d = x_ref.shape[-1]
    halo = halo_ref[0]
    xs_ref[0:POOL_HALO, :] = jnp.where(i > 0, halo, jnp.zeros_like(halo))
    xs_ref[POOL_HALO:, :] = xp_ref[0]
    pos = i * tm + lax.broadcasted_iota(I32, (tm, 1), 0)
    y_pool = None
    ya = []
    for g, w in enumerate(POOL_WINDOWS):
        cols = slice(g * POOL_GROUP_DIM, (g + 1) * POOL_GROUP_DIM)
        tok = xs_ref[POOL_HALO:, cols]
        acc = tok
        for u in range(1, w):
            acc = acc + xs_ref[POOL_HALO - u:POOL_HALO - u + tm, cols]
        cnt = jnp.minimum(pos + 1, w).astype(F32)
        pooled = acc / cnt - tok
        yg = jnp.dot(pooled.astype(BF16), pw_ref[g], preferred_element_type=F32)
        ya.append((yg * ps_ref[:, cols]).astype(BF16))
    ya = jnp.concatenate(ya, axis=-1)
    y_pool = jnp.dot(ya, wbp_ref[...], preferred_element_type=F32)
    y_attn = jnp.dot(o_ref[0], wba_ref[...], preferred_element_type=F32)
    merged = (gt_ref[0, :, 0:d].astype(F32) * y_pool + gt_ref[0, :, d:2 * d].astype(F32) * y_attn)
    mix = jnp.dot(merged.astype(BF16), wout_ref[...], preferred_element_type=F32)
    x1 = x_ref[0] + mod_ref[0, 2:3, :] * mix
    x1_ref[0] = x1
    h2 = (_rms(x1) * gain_ref[...]) * (1.0 + mod_ref[0, 4:5, :]) + mod_ref[0, 3:4, :]
    h2_ref[0] = h2.astype(h2_ref.dtype)


def _merge_out(x, xp, o, gates, mod, gain_f, pool_w, pool_scale, wbp, wba, wout, tm):
    bsz, s, d = x.shape
    const2 = lambda b, i: (0, 0)
    tile = lambda b, i: (b, i, 0)
    hb = tm // POOL_HALO
    one = pl.Buffered(1)
    return pl.pallas_call(
        functools.partial(_merge_kernel, tm=tm),
        grid=(bsz, s // tm),
        in_specs=[pl.BlockSpec((1, tm, d), tile),
                  pl.BlockSpec((1, tm, d), tile),
                  pl.BlockSpec((1, POOL_HALO, d), lambda b, i: (b, jnp.maximum(i * hb - 1, 0), 0)),
                  pl.BlockSpec((1, tm, d), tile),
                  pl.BlockSpec((1, tm, 2 * d), tile),
                  pl.BlockSpec((1, N_MOD, d), lambda b, i: (b, 0, 0)),
                  pl.BlockSpec((1, d), const2),
                  pl.BlockSpec(pool_w.shape, lambda b, i: (0, 0, 0), pipeline_mode=one),
                  pl.BlockSpec((1, d), const2),
                  pl.BlockSpec(wbp.shape, const2, pipeline_mode=one),
                  pl.BlockSpec(wba.shape, const2, pipeline_mode=one),
                  pl.BlockSpec(wout.shape, const2, pipeline_mode=one)],
        out_specs=[pl.BlockSpec((1, tm, d), tile), pl.BlockSpec((1, tm, d), tile)],
        out_shape=[jax.ShapeDtypeStruct((bsz, s, d), F32),
                   jax.ShapeDtypeStruct((bsz, s, d), BF16)],
        scratch_shapes=[pltpu.VMEM((tm + POOL_HALO, d), F32)],
        compiler_params=pltpu.CompilerParams(
            dimension_semantics=("parallel", "parallel"), vmem_limit_bytes=VMEM_LIMIT),
        name="merge_out",
    )(x, xp, xp, o, gates, mod, gain_f, pool_w, pool_scale, wbp, wba, wout)


def _staircase():
    return [(a, b) for a in range(PEER_TOPK) for b in range(PEER_TOPK)
            if (a + 1) * (b + 1) <= PEER_TOPK]


def _tree(op, xs):
    xs = list(xs)
    while len(xs) > 1:
        xs = [op(xs[i], xs[i + 1]) if i + 1 < len(xs) else xs[i] for i in range(0, len(xs), 2)]
    return xs[0]


def _route_kernel(h2_ref, wq_ref, sk_ref, isel_ref, jsel_ref, g_ref,
                  slab_ref, km_ref, sv_ref, si_ref, oi_ref, oj_ref, og_ref, *, tb):
    n_slab = tb // LANES
    big = 1e9
    h2 = h2_ref[...]
    cands = _staircase()

    def head(hh, _):
        q = jnp.dot(h2, wq_ref[hh], preferred_element_type=F32).astype(BF16)
        for half in range(2):
            qh = q[:, half * PEER_HALF:(half + 1) * PEER_HALF]
            st = lax.dot_general(sk_ref[half], qh, (((1,), (1,)), ((), ())),
                                 preferred_element_type=F32)
            for c in range(n_slab):
                slab_ref[c * ROUTE_PITCH:c * ROUTE_PITCH + N_KEYS, :] = (
                    st[:, c * LANES:(c + 1) * LANES])
            for k in range(N_KEYS):
                km_ref[k] = slab_ref[pl.ds(k, n_slab, stride=ROUTE_PITCH), :]

            def extract(r, pos_prev):
                part = [None] * 8
                for k in range(N_KEYS):
                    v = jnp.where(pos_prev == float(k), -jnp.inf, km_ref[k])
                    km_ref[k] = v
                    if part[k % 8] is None:
                        part[k % 8] = (v, jnp.full_like(v, float(k)))
                    else:
                        bm, bk = part[k % 8]
                        part[k % 8] = (jnp.maximum(bm, v), jnp.where(v > bm, float(k), bk))

                def lower_key_wins(x, y):
                    take_y = (y[0] > x[0]) | ((y[0] == x[0]) & (y[1] < x[1]))
                    return jnp.where(take_y, y[0], x[0]), jnp.where(take_y, y[1], x[1])

                m, pos = _tree(lower_key_wins, part)
                sv_ref[half, r] = m
                si_ref[half, r] = pos
                return pos

            lax.fori_loop(0, PEER_TOPK, extract, jnp.full((n_slab, LANES), -1.0, F32))

        sv0 = [sv_ref[0, a] for a in range(PEER_TOPK)]
        sv1 = [sv_ref[1, b] for b in range(PEER_TOPK)]
        cand = [sv0[a] + sv1[b] for a, b in cands]
        code = [float(a * PEER_TOPK + b) for a, b in cands]
        top_s, top_pos = [], []
        for _ in range(PEER_TOPK):
            m = _tree(jnp.maximum, cand)
            pos = _tree(jnp.minimum, [jnp.where(c == m, cd, big) for c, cd in zip(cand, code)])
            top_s.append(m)
            top_pos.append(pos)
            cand = [jnp.where(pos == cd, -jnp.inf, c) for c, cd in zip(cand, code)]
        si0 = [si_ref[0, a] for a in range(PEER_TOPK)]
        si1 = [si_ref[1, b] for b in range(PEER_TOPK)]
        e = [jnp.exp(t - top_s[0]) for t in top_s]
        denom = _tree(jnp.add, e)
        for r in range(PEER_TOPK):
            a_sel = jnp.floor(top_pos[r] * (1.0 / PEER_TOPK))
            b_sel = top_pos[r] - a_sel * PEER_TOPK
            i_sel = si0[0]
            j_sel = si1[0]
            for a in range(1, PEER_TOPK):
                i_sel = jnp.where(a_sel == float(a), si0[a], i_sel)
                j_sel = jnp.where(b_sel == float(a), si1[a], j_sel)
            row = hh * PEER_TOPK + r
            oi_ref[row] = i_sel
            oj_ref[row] = j_sel
            og_ref[row] = e[r] / denom
        return 0

    lax.fori_loop(0, PEER_HEADS, head, 0)

    for src, dst in ((oi_ref, isel_ref), (oj_ref, jsel_ref), (og_ref, g_ref)):
        for p in range(N_PAIRS):
            slab_ref[pl.ds(p, n_slab, stride=ROUTE_PITCH), :] = src[p]
        for c in range(n_slab):
            blk = slab_ref[c * ROUTE_PITCH:c * ROUTE_PITCH + N_PAIRS, :].T
            dst[c * LANES:(c + 1) * LANES, :] = blk.astype(dst.dtype)


def _peer_route(h2, wq, sk, tb):
    t, d = h2.shape
    tile = lambda i: (i, 0)
    one = pl.Buffered(1)
    n_slab = tb // LANES
    vregs = lambda n: pltpu.VMEM((n, n_slab, LANES), F32)
    return pl.pallas_call(
        functools.partial(_route_kernel, tb=tb),
        grid=(t // tb,),
        in_specs=[pl.BlockSpec((tb, d), tile),
                  pl.BlockSpec(wq.shape, lambda i: (0, 0, 0), pipeline_mode=one),
                  pl.BlockSpec(sk.shape, lambda i: (0, 0, 0), pipeline_mode=one)],
        out_specs=[pl.BlockSpec((tb, N_PAIRS), tile)] * 3,
        out_shape=[jax.ShapeDtypeStruct((t, N_PAIRS), I32),
                   jax.ShapeDtypeStruct((t, N_PAIRS), I32),
                   jax.ShapeDtypeStruct((t, N_PAIRS), F32)],
        scratch_shapes=[pltpu.VMEM((n_slab * ROUTE_PITCH, LANES), F32),
                        vregs(N_KEYS),
                        pltpu.VMEM((2, PEER_TOPK, n_slab, LANES), F32),
                        pltpu.VMEM((2, PEER_TOPK, n_slab, LANES), F32),
                        vregs(N_PAIRS), vregs(N_PAIRS), vregs(N_PAIRS)],
        compiler_params=pltpu.CompilerParams(
            dimension_semantics=("parallel",), vmem_limit_bytes=VMEM_LIMIT),
        name="peer_route",
    )(h2, wq, sk)


def _ffn_kernel(h2_ref, isel_ref, jsel_ref, g_ref, dnt_ref, up_ref, out_ref, w_ref, *, tb, ec):
    e = pl.program_id(1)

    @pl.when(e == 0)
    def _build():
        out_ref[...] = jnp.zeros_like(out_ref)
        key = lax.broadcasted_iota(I32, (N_KEYS, N_PAIRS), 0)

        def tok(t, _):
            ii = jnp.broadcast_to(isel_ref[pl.ds(t, 1), :], (N_KEYS, N_PAIRS))
            jj = jnp.broadcast_to(jsel_ref[pl.ds(t, 1), :], (N_KEYS, N_PAIRS))
            gg = jnp.broadcast_to(g_ref[pl.ds(t, 1), :], (N_KEYS, N_PAIRS))
            a = jnp.where(ii == key, 1.0, 0.0).astype(BF16)
            b = jnp.where(jj == key, gg, 0.0).astype(BF16)
            w_ref[pl.ds(pl.multiple_of(t * W_PITCH, 8), N_KEYS), :] = lax.dot_general(
                a, b, (((1,), (1,)), ((), ())), preferred_element_type=F32)
            return 0

        lax.fori_loop(0, tb, tok, 0, unroll=32)

    h2 = h2_ref[...]
    tile0 = e * (ec // LANES)
    z = []
    for c in range(ec // MXU_DIM):
        cols = slice(c * MXU_DIM, (c + 1) * MXU_DIM)
        pre = jnp.dot(h2, dnt_ref[:, cols], preferred_element_type=F32)
        w = jnp.concatenate(
            [w_ref[pl.ds(tile0 + c * (MXU_DIM // LANES) + u, tb, stride=W_PITCH), :]
             for u in range(MXU_DIM // LANES)], axis=-1)
        z.append((w * _gelu(pre)).astype(BF16))
    out_ref[...] += jnp.dot(jnp.concatenate(z, axis=-1), up_ref[...],
                            preferred_element_type=F32)


def _peer_ffn(h2, isel, jsel, g, down_t, up, tb, ec):
    t, d = h2.shape
    n_exp = up.shape[0]
    tile = lambda i, e: (i, 0)
    return pl.pallas_call(
        functools.partial(_ffn_kernel, tb=tb, ec=ec),
        grid=(t // tb, n_exp // ec),
        in_specs=[pl.BlockSpec((tb, d), tile),
                  pl.BlockSpec((tb, N_PAIRS), tile),
                  pl.BlockSpec((tb, N_PAIRS), tile),
                  pl.BlockSpec((tb, N_PAIRS), tile),
                  pl.BlockSpec((d, ec), lambda i, e: (0, e)),
                  pl.BlockSpec((ec, d), lambda i, e: (e, 0))],
        out_specs=pl.BlockSpec((tb, d), tile),
        out_shape=jax.ShapeDtypeStruct((t, d), F32),
        scratch_shapes=[pltpu.VMEM((tb * W_PITCH, LANES), F32)],
        compiler_params=pltpu.CompilerParams(
            dimension_semantics=("parallel", "arbitrary"), vmem_limit_bytes=VMEM_LIMIT),
        name="peer_ffn",
    )(h2, isel, jsel, g, down_t, up)


def _finish_kernel(x1_ref, y_ref, gate_ref, fg_ref, o_ref):
    x2 = x1_ref[0] + gate_ref[0] * y_ref[0]
    o_ref[0] = _rms(x2) * fg_ref[...]


def _finish(x1, y, gate_f, final_gain, tm):
    bsz, s, d = x1.shape
    tile = lambda b, i: (b, i, 0)
    return pl.pallas_call(
        _finish_kernel,
        grid=(bsz, s // tm),
        in_specs=[pl.BlockSpec((1, tm, d), tile),
                  pl.BlockSpec((1, tm, d), tile),
                  pl.BlockSpec((1, 1, d), lambda b, i: (b, 0, 0)),
                  pl.BlockSpec((1, d), lambda b, i: (0, 0))],
        out_specs=pl.BlockSpec((1, tm, d), tile),
        out_shape=jax.ShapeDtypeStruct((bsz, s, d), F32),
        compiler_params=pltpu.CompilerParams(
            dimension_semantics=("parallel", "parallel"), vmem_limit_bytes=VMEM_LIMIT),
        name="finish",
    )(x1, y, gate_f, final_gain)


def _layer(x, posf, invf, mod, l, p, tm, tq, tb_route, tb_ffn, ec):
    bsz, s, d = x.shape
    lam_init = 0.8 - 0.6 * math.exp(-0.3 * l)
    qt, k, vt, xp, gates = _mixer_in(
        x, posf, mod, p['norm_mix_gain'][l][None], invf, p['w_in'][l].astype(BF16),
        p['w_gate'][l].astype(BF16), p['b_gate'][l][None], tm)
    lam_vecs = jnp.stack([p['lambda_q1'][l], p['lambda_k1'][l],
                          p['lambda_q2'][l], p['lambda_k2'][l]])
    o = _diff_attn(lam_vecs, qt, k, vt, p['attn_subln_gain'][l][None], lam_init, tq, 4)
    x1, h2 = _merge_out(
        x, xp, o, gates, mod, p['norm_ffn_gain'][l][None], p['pool_w'][l].astype(BF16),
        p['pool_scale'][l][None], p['w_branch_pool'][l].astype(BF16),
        p['w_branch_attn'][l].astype(BF16), p['w_out'][l].astype(BF16), tm)
    h2f = h2.reshape(bsz * s, d)
    wq = p['peer_w_query'][l].astype(BF16).reshape(d, PEER_HEADS, 2 * PEER_HALF).transpose(1, 0, 2)
    isel, jsel, g = _peer_route(h2f, wq, p['peer_sub_keys'][l].astype(BF16), tb_route)
    return h2f, isel, jsel, g, x1


def kernel(x, c, positions, norm_mix_gain, norm_ffn_gain, w_ada, b_ada, w_in, w_gate, b_gate,
           pool_w, pool_scale, lambda_q1, lambda_k1, lambda_q2, lambda_k2, attn_subln_gain,
           w_branch_pool, w_branch_attn, w_out, peer_w_query, peer_sub_keys, peer_down, peer_up,
           final_norm_gain):
    bsz, s, d = x.shape
    depth = w_in.shape[0]
    assert depth == 1, "the driver below runs one layer and then the final RMSNorm"
    p = dict(norm_mix_gain=norm_mix_gain, norm_ffn_gain=norm_ffn_gain, w_in=w_in, w_gate=w_gate,
             b_gate=b_gate, pool_w=pool_w, pool_scale=pool_scale, lambda_q1=lambda_q1,
             lambda_k1=lambda_k1, lambda_q2=lambda_q2, lambda_k2=lambda_k2,
             attn_subln_gain=attn_subln_gain, w_branch_pool=w_branch_pool,
             w_branch_attn=w_branch_attn, w_out=w_out, peer_w_query=peer_w_query,
             peer_sub_keys=peer_sub_keys)
    tm = min(512, s)
    tq = min(256, s)
    tb_route = 1024
    tb_ffn = 512
    ec = 1024
    inv_freq = ROPE_THETA ** (-jnp.arange(0, ROT_DIM, 2, dtype=F32) / ROT_DIM)
    invf = jnp.tile(inv_freq, LANES // (ROT_DIM // 2))[None]
    posf = positions.astype(F32)[..., None]
    l = 0
    mod = _ada_mod(c, w_ada[l], b_ada[l]).reshape(bsz, N_MOD, d)
    h2f, isel, jsel, g, x1 = _layer(x, posf, invf, mod, l, p, tm, tq, tb_route, tb_ffn, ec)
    down_t = peer_down[l].astype(BF16).T
    y = _peer_ffn(h2f, isel, jsel, g, down_t, peer_up[l].astype(BF16), tb_ffn, ec)
    return _finish(x1, y.reshape(bsz, s, d), mod[:, 5:6, :], final_norm_gain[None], tm)
```

```python
import functools
import math

import jax
import jax.numpy as jnp
from jax import lax
from jax.experimental import pallas as pl
from jax.experimental.pallas import tpu as pltpu

F32 = jnp.float32
BF16 = jnp.bfloat16
I32 = jnp.int32

EPS = 1e-6
NEG_INF = -1e30

N_ATTN_HEADS = 8
ATTN_HEAD_DIM = 64
ATTN_V_DIM = 128
ROT_DIM = 16
ROPE_THETA = 500000.0
POOL_WINDOWS = (2, 4, 8, 16)
POOL_GROUP_DIM = 256
POOL_HALO = 16
N_MOD = 6

PEER_HEADS = 8
N_KEYS = 128
PEER_HALF = 128
PEER_TOPK = 16
N_PAIRS = PEER_HEADS * PEER_TOPK

LANES = 128
MXU_DIM = 256
VMEM_LIMIT = 56 * 1024 * 1024

W_PITCH = 136
ROUTE_PITCH = 136


def _rms(x, eps=EPS):
    return x * lax.rsqrt(jnp.mean(x * x, axis=-1, keepdims=True) + eps)


def _gelu(x):
    return 0.5 * x * (1.0 + lax.erf(x * math.sqrt(0.5)))


def _ada_kernel(c_ref, w_ref, b_ref, o_ref):
    ca = jax.nn.silu(c_ref[...])
    o_ref[...] = jnp.dot(ca, w_ref[...], preferred_element_type=F32,
                         precision=lax.Precision.HIGHEST) + b_ref[...]


def _ada_mod(c, w, b):
    bsz, d = c.shape
    n = w.shape[1]
    tn = 1536
    return pl.pallas_call(
        _ada_kernel,
        grid=(n // tn,),
        in_specs=[pl.BlockSpec((bsz, d), lambda j: (0, 0)),
                  pl.BlockSpec((d, tn), lambda j: (0, j)),
                  pl.BlockSpec((1, tn), lambda j: (0, j))],
        out_specs=pl.BlockSpec((bsz, tn), lambda j: (0, j)),
        out_shape=jax.ShapeDtypeStruct((bsz, n), F32),
        compiler_params=pltpu.CompilerParams(vmem_limit_bytes=VMEM_LIMIT),
        name="ada_mod",
    )(c, w, b.reshape(1, n))


def _mixer_in_kernel(x_ref, pos_ref, mod_ref, gain_ref, invf_ref, win_ref, wg_ref, bg_ref,
                     qt_ref, k_ref, vt_ref, xp_ref, gt_ref):
    d = x_ref.shape[-1]
    x = x_ref[0]
    shift = mod_ref[0, 0:1, :]
    scale = mod_ref[0, 1:2, :]
    h = (_rms(x) * gain_ref[...]) * (1.0 + scale) + shift
    hb = h.astype(BF16)

    ang = pos_ref[0] * invf_ref[...]
    cos = jnp.cos(ang)
    sin = jnp.sin(ang)
    lane = lax.broadcasted_iota(I32, (1, LANES), 1) % ATTN_HEAD_DIM
    half = ROT_DIM // 2
    cm = jnp.where(lane < ROT_DIM, cos, 1.0)
    sa = jnp.where(lane < half, -sin, 0.0)
    sb = jnp.where((lane >= half) & (lane < ROT_DIM), sin, 0.0)

    def rope(t):
        return (t * cm + pltpu.roll(t, LANES - half, 1) * sa + pltpu.roll(t, half, 1) * sb)

    p = jnp.dot(hb, win_ref[:, 0:d], preferred_element_type=F32)
    for hh in range(d // LANES):
        sl = slice(hh * LANES, (hh + 1) * LANES)
        qt_ref[0, sl, :] = (rope(p[:, sl]) * ATTN_HEAD_DIM ** -0.5).T.astype(qt_ref.dtype)
    p = jnp.dot(hb, win_ref[:, d:2 * d], preferred_element_type=F32)
    for hh in range(d // LANES):
        sl = slice(hh * LANES, (hh + 1) * LANES)
        k_ref[0, :, sl] = rope(p[:, sl]).astype(k_ref.dtype)
    p = jnp.dot(hb, win_ref[:, 2 * d:3 * d], preferred_element_type=F32)
    for hh in range(d // LANES):
        sl = slice(hh * LANES, (hh + 1) * LANES)
        vt_ref[0, sl, :] = p[:, sl].T.astype(vt_ref.dtype)
    xp_ref[0] = jnp.dot(hb, win_ref[:, 3 * d:4 * d], preferred_element_type=F32)
    for c in range(2):
        g = jnp.dot(hb, wg_ref[:, c * d:(c + 1) * d], preferred_element_type=F32)
        gt_ref[0, :, c * d:(c + 1) * d] = jax.nn.sigmoid(
            g + bg_ref[:, c * d:(c + 1) * d]).astype(gt_ref.dtype)


def _mixer_in(x, posf, mod, gain, invf, w_in, w_gate, b_gate, tm):
    bsz, s, d = x.shape
    const = lambda b, i: (0, 0)
    tile = lambda b, i: (b, i, 0)
    ttile = lambda b, i: (b, 0, i)
    one = pl.Buffered(1)
    return pl.pallas_call(
        _mixer_in_kernel,
        grid=(bsz, s // tm),
        in_specs=[pl.BlockSpec((1, tm, d), tile),
                  pl.BlockSpec((1, tm, 1), tile),
                  pl.BlockSpec((1, N_MOD, d), lambda b, i: (b, 0, 0)),
                  pl.BlockSpec((1, d), const),
                  pl.BlockSpec((1, LANES), const),
                  pl.BlockSpec(w_in.shape, const, pipeline_mode=one),
                  pl.BlockSpec(w_gate.shape, const, pipeline_mode=one),
                  pl.BlockSpec((1, 2 * d), const)],
        out_specs=[pl.BlockSpec((1, d, tm), ttile), pl.BlockSpec((1, tm, d), tile),
                   pl.BlockSpec((1, d, tm), ttile), pl.BlockSpec((1, tm, d), tile),
                   pl.BlockSpec((1, tm, 2 * d), tile)],
        out_shape=[jax.ShapeDtypeStruct((bsz, d, s), BF16),
                   jax.ShapeDtypeStruct((bsz, s, d), BF16),
                   jax.ShapeDtypeStruct((bsz, d, s), BF16),
                   jax.ShapeDtypeStruct((bsz, s, d), F32),
                   jax.ShapeDtypeStruct((bsz, s, 2 * d), BF16)],
        compiler_params=pltpu.CompilerParams(
            dimension_semantics=("parallel", "parallel"), vmem_limit_bytes=VMEM_LIMIT),
        name="mixer_in",
    )(x, posf, mod, gain, invf, w_in, w_gate, b_gate)


def _attn_kernel(lam_ref, qt_ref, k_ref, vt_ref, gain_ref, o_ref, *, nblk, tq, hps, lam_init):
    sub = lax.broadcasted_iota(I32, (LANES, tq), 0)
    zero = jnp.zeros((LANES, tq), BF16)
    qq = []
    for h in range(hps):
        qt = qt_ref[0, h * LANES:(h + 1) * LANES, :]
        qq.append(jnp.concatenate([jnp.where(sub < ATTN_HEAD_DIM, qt, zero),
                                   jnp.where(sub >= ATTN_HEAD_DIM, qt, zero)], axis=1))
    m = [jnp.full((1, 2 * tq), NEG_INF, F32)] * hps
    l = [jnp.zeros((1, 2 * tq), F32)] * hps
    acc = [jnp.zeros((ATTN_V_DIM, 2 * tq), F32)] * hps

    def scores(j, h):
        return jnp.dot(k_ref[0, j * tq:(j + 1) * tq, h * LANES:(h + 1) * LANES], qq[h],
                       preferred_element_type=F32)

    s_next = [scores(0, h) for h in range(hps)]
    for j in range(nblk):
        for h in range(hps):
            s = s_next[h]
            if j + 1 < nblk:
                s_next[h] = scores(j + 1, h)
            if j == nblk - 1:
                kv = lax.broadcasted_iota(I32, s.shape, 0)
                qr = lax.broadcasted_iota(I32, s.shape, 1) % tq
                s = jnp.where(qr >= kv, s, NEG_INF)
            m_new = jnp.maximum(m[h], jnp.max(s, axis=0, keepdims=True))
            alpha = jnp.exp(m[h] - m_new)
            p = jnp.exp(s - m_new)
            l[h] = alpha * l[h] + jnp.sum(p, axis=0, keepdims=True)
            vt = vt_ref[0, h * LANES:(h + 1) * LANES, j * tq:(j + 1) * tq]
            acc[h] = alpha * acc[h] + jnp.dot(vt, p.astype(BF16),
                                              preferred_element_type=F32)
            m[h] = m_new

    lv = lam_ref[...]
    lam = (jnp.exp(jnp.sum(lv[0:1] * lv[1:2], axis=-1, keepdims=True))
           - jnp.exp(jnp.sum(lv[2:3] * lv[3:4], axis=-1, keepdims=True)) + lam_init)
    for h in range(hps):
        cols = slice(h * LANES, (h + 1) * LANES)
        ot = acc[h] * (1.0 / l[h])
        o = (ot[:, :tq] - lam * ot[:, tq:]).T
        o_ref[0, :, cols] = (_rms(o) * gain_ref[:, cols] * (1.0 - lam_init)).astype(o_ref.dtype)


def _diff_attn(lam_vecs, qt, k, vt, gain, lam_init, tq, hps):
    bsz, s, d = k.shape
    w = hps * LANES
    outs = []
    for qi in range(s // tq):
        kv_len = (qi + 1) * tq
        outs.append(pl.pallas_call(
            functools.partial(_attn_kernel, nblk=qi + 1, tq=tq, hps=hps, lam_init=lam_init),
            grid=(bsz, d // w),
            in_specs=[pl.BlockSpec(lam_vecs.shape, lambda b, h: (0, 0)),
                      pl.BlockSpec((1, w, tq), lambda b, h, qi=qi: (b, h, qi)),
                      pl.BlockSpec((1, kv_len, w), lambda b, h: (b, 0, h)),
                      pl.BlockSpec((1, w, kv_len), lambda b, h: (b, h, 0)),
                      pl.BlockSpec((1, w), lambda b, h: (0, h))],
            out_specs=pl.BlockSpec((1, tq, w), lambda b, h: (b, 0, h)),
            out_shape=jax.ShapeDtypeStruct((bsz, tq, d), BF16),
            compiler_params=pltpu.CompilerParams(
                dimension_semantics=("parallel", "parallel"), vmem_limit_bytes=VMEM_LIMIT),
            name=f"diff_attn_q{qi}",
        )(lam_vecs, qt, k, vt, gain))
    return jnp.concatenate(outs, axis=1)


def _merge_kernel(x_ref, xp_ref, halo_ref, o_ref, gt_ref, mod_ref, gain_ref, pw_ref, ps_ref,
                  wbp_ref, wba_ref, wout_ref, x1_ref, h2_ref, xs_ref, *, tm):
    i = pl.program_id(1)
    d = x_ref.shape[-1]
    halo = halo_ref[0]
    xs_ref[0:POOL_HALO, :] = jnp.where(i > 0, halo, jnp.zeros_like(halo))
    xs_ref[POOL_HALO:, :] = xp_ref[0]
    pos = i * tm + lax.broadcasted_iota(I32, (tm, 1), 0)
    y_pool = None
    ya = []
    for g, w in enumerate(POOL_WINDOWS):
        cols = slice(g * POOL_GROUP_DIM, (g + 1) * POOL_GROUP_DIM)
        tok = xs_ref[POOL_HALO:, cols]
        acc = tok
        for u in range(1, w):
            acc = acc + xs_ref[POOL_HALO - u:POOL_HALO - u + tm, cols]
        cnt = jnp.minimum(pos + 1, w).astype(F32)
        pooled = acc / cnt - tok
        yg = jnp.dot(pooled.astype(BF16), pw_ref[g], preferred_element_type=F32)
        ya.append((yg * ps_ref[:, cols]).astype(BF16))
    ya = jnp.concatenate(ya, axis=-1)
    y_pool = jnp.dot(ya, wbp_ref[...], preferred_element_type=F32)
    y_attn = jnp.dot(o_ref[0], wba_ref[...], preferred_element_type=F32)
    merged = (gt_ref[0, :, 0:d].astype(F32) * y_pool + gt_ref[0, :, d:2 * d].astype(F32) * y_attn)
    mix = jnp.dot(merged.astype(BF16), wout_ref[...], preferred_element_type=F32)
    x1 = x_ref[0] + mod_ref[0, 2:3, :] * mix
    x1_ref[0] = x1
    h2 = (_rms(x1) * gain_ref[...]) * (1.0 + mod_ref[0, 4:5, :]) + mod_ref[0, 3:4, :]
    h2_ref[0] = h2.astype(h2_ref.dtype)


def _merge_out(x, xp, o, gates, mod, gain_f, pool_w, pool_scale, wbp, wba, wout, tm):
    bsz, s, d = x.shape
    const2 = lambda b, i: (0, 0)
    tile = lambda b, i: (b, i, 0)
    hb = tm // POOL_HALO
    one = pl.Buffered(1)
    return pl.pallas_call(
        functools.partial(_merge_kernel, tm=tm),
        grid=(bsz, s // tm),
        in_specs=[pl.BlockSpec((1, tm, d), tile),
                  pl.BlockSpec((1, tm, d), tile),
                  pl.BlockSpec((1, POOL_HALO, d), lambda b, i: (b, jnp.maximum(i * hb - 1, 0), 0)),
                  pl.BlockSpec((1, tm, d), tile),
                  pl.BlockSpec((1, tm, 2 * d), tile),
                  pl.BlockSpec((1, N_MOD, d), lambda b, i: (b, 0, 0)),
                  pl.BlockSpec((1, d), const2),
                  pl.BlockSpec(pool_w.shape, lambda b, i: (0, 0, 0), pipeline_mode=one),
                  pl.BlockSpec((1, d), const2),
                  pl.BlockSpec(wbp.shape, const2, pipeline_mode=one),
                  pl.BlockSpec(wba.shape, const2, pipeline_mode=one),
                  pl.BlockSpec(wout.shape, const2, pipeline_mode=one)],
        out_specs=[pl.BlockSpec((1, tm, d), tile), pl.BlockSpec((1, tm, d), tile)],
        out_shape=[jax.ShapeDtypeStruct((bsz, s, d), F32),
                   jax.ShapeDtypeStruct((bsz, s, d), BF16)],
        scratch_shapes=[pltpu.VMEM((tm + POOL_HALO, d), F32)],
        compiler_params=pltpu.CompilerParams(
            dimension_semantics=("parallel", "parallel"), vmem_limit_bytes=VMEM_LIMIT),
        name="merge_out",
    )(x, xp, xp, o, gates, mod, gain_f, pool_w, pool_scale, wbp, wba, wout)


def _staircase():
    return [(a, b) for a in range(PEER_TOPK) for b in range(PEER_TOPK)
            if (a + 1) * (b + 1) <= PEER_TOPK]


def _tree(op, xs):
    xs = list(xs)
    while len(xs) > 1:
        xs = [op(xs[i], xs[i + 1]) if i + 1 < len(xs) else xs[i] for i in range(0, len(xs), 2)]
    return xs[0]


def _route_kernel(h2_ref, wq_ref, sk_ref, isel_ref, jsel_ref, g_ref,
                  slab_ref, km_ref, sv_ref, si_ref, oi_ref, oj_ref, og_ref, *, tb):
    n_slab = tb // LANES
    big = 1e9
    h2 = h2_ref[...]
    cands = _staircase()

    def head(hh, _):
        q = jnp.dot(h2, wq_ref[hh], preferred_element_type=F32).astype(BF16)
        for half in range(2):
            qh = q[:, half * PEER_HALF:(half + 1) * PEER_HALF]
            st = lax.dot_general(sk_ref[half], qh, (((1,), (1,)), ((), ())),
                                 preferred_element_type=F32)
            for c in range(n_slab):
                slab_ref[c * ROUTE_PITCH:c * ROUTE_PITCH + N_KEYS, :] = (
                    st[:, c * LANES:(c + 1) * LANES])
            for k in range(N_KEYS):
                km_ref[k] = slab_ref[pl.ds(k, n_slab, stride=ROUTE_PITCH), :]

            def extract(r, pos_prev):
                part = [None] * 8
                for k in range(N_KEYS):
                    v = jnp.where(pos_prev == float(k), -jnp.inf, km_ref[k])
                    km_ref[k] = v
                    if part[k % 8] is None:
                        part[k % 8] = (v, jnp.full_like(v, float(k)))
                    else:
                        bm, bk = part[k % 8]
                        part[k % 8] = (jnp.maximum(bm, v), jnp.where(v > bm, float(k), bk))

                def lower_key_wins(x, y):
                    take_y = (y[0] > x[0]) | ((y[0] == x[0]) & (y[1] < x[1]))
                    return jnp.where(take_y, y[0], x[0]), jnp.where(take_y, y[1], x[1])

                m, pos = _tree(lower_key_wins, part)
                sv_ref[half, r] = m
                si_ref[half, r] = pos
                return pos

            lax.fori_loop(0, PEER_TOPK, extract, jnp.full((n_slab, LANES), -1.0, F32))

        sv0 = [sv_ref[0, a] for a in range(PEER_TOPK)]
        sv1 = [sv_ref[1, b] for b in range(PEER_TOPK)]
        cand = [sv0[a] + sv1[b] for a, b in cands]
        code = [float(a * PEER_TOPK + b) for a, b in cands]
        top_s, top_pos = [], []
        for _ in range(PEER_TOPK):
            m = _tree(jnp.maximum, cand)
            pos = _tree(jnp.minimum, [jnp.where(c == m, cd, big) for c, cd in zip(cand, code)])
            top_s.append(m)
            top_pos.append(pos)
            cand = [jnp.where(pos == cd, -jnp.inf, c) for c, cd in zip(cand, code)]
        si0 = [si_ref[0, a] for a in range(PEER_TOPK)]
        si1 = [si_ref[1, b] for b in range(PEER_TOPK)]
        e = [jnp.exp(t - top_s[0]) for t in top_s]
        denom = _tree(jnp.add, e)
        for r in range(PEER_TOPK):
            a_sel = jnp.floor(top_pos[r] * (1.0 / PEER_TOPK))
            b_sel = top_pos[r] - a_sel * PEER_TOPK
            i_sel = si0[0]
            j_sel = si1[0]
            for a in range(1, PEER_TOPK):
                i_sel = jnp.where(a_sel == float(a), si0[a], i_sel)
                j_sel = jnp.where(b_sel == float(a), si1[a], j_sel)
            row = hh * PEER_TOPK + r
            oi_ref[row] = i_sel
            oj_ref[row] = j_sel
            og_ref[row] = e[r] / denom
        return 0

    lax.fori_loop(0, PEER_HEADS, head, 0)

    for src, dst in ((oi_ref, isel_ref), (oj_ref, jsel_ref), (og_ref, g_ref)):
        for p in range(N_PAIRS):
            slab_ref[pl.ds(p, n_slab, stride=ROUTE_PITCH), :] = src[p]
        for c in range(n_slab):
            blk = slab_ref[c * ROUTE_PITCH:c * ROUTE_PITCH + N_PAIRS, :].T
            dst[c * LANES:(c + 1) * LANES, :] = blk.astype(dst.dtype)


def _peer_route(h2, wq, sk, tb):
    t, d = h2.shape
    tile = lambda i: (i, 0)
    one = pl.Buffered(1)
    n_slab = tb // LANES
    vregs = lambda n: pltpu.VMEM((n, n_slab, LANES), F32)
    return pl.pallas_call(
        functools.partial(_route_kernel, tb=tb),
        grid=(t // tb,),
        in_specs=[pl.BlockSpec((tb, d), tile),
                  pl.BlockSpec(wq.shape, lambda i: (0, 0, 0), pipeline_mode=one),
                  pl.BlockSpec(sk.shape, lambda i: (0, 0, 0), pipeline_mode=one)],
        out_specs=[pl.BlockSpec((tb, N_PAIRS), tile)] * 3,
        out_shape=[jax.ShapeDtypeStruct((t, N_PAIRS), I32),
                   jax.ShapeDtypeStruct((t, N_PAIRS), I32),
                   jax.ShapeDtypeStruct((t, N_PAIRS), F32)],
        scratch_shapes=[pltpu.VMEM((n_slab * ROUTE_PITCH, LANES), F32),
                        vregs(N_KEYS),
                        pltpu.VMEM((2, PEER_TOPK, n_slab, LANES), F32),
                        pltpu.VMEM((2, PEER_TOPK, n_slab, LANES), F32),
                        vregs(N_PAIRS), vregs(N_PAIRS), vregs(N_PAIRS)],
        compiler_params=pltpu.CompilerParams(
            dimension_semantics=("parallel",), vmem_limit_bytes=VMEM_LIMIT),
        name="peer_route",
    )(h2, wq, sk)


def _ffn_kernel(h2_ref, isel_ref, jsel_ref, g_ref, dnt_ref, up_ref, out_ref, w_ref, *, tb, ec):
    e = pl.program_id(1)

    @pl.when(e == 0)
    def _build():
        out_ref[...] = jnp.zeros_like(out_ref)
        key = lax.broadcasted_iota(I32, (N_KEYS, N_PAIRS), 0)

        def tok(t, _):
            ii = jnp.broadcast_to(isel_ref[pl.ds(t, 1), :], (N_KEYS, N_PAIRS))
            jj = jnp.broadcast_to(jsel_ref[pl.ds(t, 1), :], (N_KEYS, N_PAIRS))
            gg = jnp.broadcast_to(g_ref[pl.ds(t, 1), :], (N_KEYS, N_PAIRS))
            a = jnp.where(ii == key, 1.0, 0.0).astype(BF16)
            b = jnp.where(jj == key, gg, 0.0).astype(BF16)
            w_ref[pl.ds(pl.multiple_of(t * W_PITCH, 8), N_KEYS), :] = lax.dot_general(
                a, b, (((1,), (1,)), ((), ())), preferred_element_type=F32)
            return 0

        lax.fori_loop(0, tb, tok, 0, unroll=32)

    h2 = h2_ref[...]
    tile0 = e * (ec // LANES)
    z = []
    for c in range(ec // MXU_DIM):
        cols = slice(c * MXU_DIM, (c + 1) * MXU_DIM)
        pre = jnp.dot(h2, dnt_ref[:, cols], preferred_element_type=F32)
        w = jnp.concatenate(
            [w_ref[pl.ds(tile0 + c * (MXU_DIM // LANES) + u, tb, stride=W_PITCH), :]
             for u in range(MXU_DIM // LANES)], axis=-1)
        z.append((w * _gelu(pre)).astype(BF16))
    out_ref[...] += jnp.dot(jnp.concatenate(z, axis=-1), up_ref[...],
                            preferred_element_type=F32)


def _peer_ffn(h2, isel, jsel, g, down_t, up, tb, ec):
    t, d = h2.shape
    n_exp = up.shape[0]
    tile = lambda i, e: (i, 0)
    return pl.pallas_call(
        functools.partial(_ffn_kernel, tb=tb, ec=ec),
        grid=(t // tb, n_exp // ec),
        in_specs=[pl.BlockSpec((tb, d), tile),
                  pl.BlockSpec((tb, N_PAIRS), tile),
                  pl.BlockSpec((tb, N_PAIRS), tile),
                  pl.BlockSpec((tb, N_PAIRS), tile),
                  pl.BlockSpec((d, ec), lambda i, e: (0, e)),
                  pl.BlockSpec((ec, d), lambda i, e: (e, 0))],
        out_specs=pl.BlockSpec((tb, d), tile),
        out_shape=jax.ShapeDtypeStruct((t, d), F32),
        scratch_shapes=[pltpu.VMEM((tb * W_PITCH, LANES), F32)],
        compiler_params=pltpu.CompilerParams(
            dimension_semantics=("parallel", "arbitrary"), vmem_limit_bytes=VMEM_LIMIT),
        name="peer_ffn",
    )(h2, isel, jsel, g, down_t, up)


def _finish_kernel(x1_ref, y_ref, gate_ref, fg_ref, o_ref):
    x2 = x1_ref[0] + gate_ref[0] * y_ref[0]
    o_ref[0] = _rms(x2) * fg_ref[...]


def _finish(x1, y, gate_f, final_gain, tm):
    bsz, s, d = x1.shape
    tile = lambda b, i: (b, i, 0)
    return pl.pallas_call(
        _finish_kernel,
        grid=(bsz, s // tm),
        in_specs=[pl.BlockSpec((1, tm, d), tile),
                  pl.BlockSpec((1, tm, d), tile),
                  pl.BlockSpec((1, 1, d), lambda b, i: (b, 0, 0)),
                  pl.BlockSpec((1, d), lambda b, i: (0, 0))],
        out_specs=pl.BlockSpec((1, tm, d), tile),
        out_shape=jax.ShapeDtypeStruct((bsz, s, d), F32),
        compiler_params=pltpu.CompilerParams(
            dimension_semantics=("parallel", "parallel"), vmem_limit_bytes=VMEM_LIMIT),
        name="finish",
    )(x1, y, gate_f, final_gain)


def _layer(x, posf, invf, mod, l, p, tm, tq, tb_route, tb_ffn, ec):
    bsz, s, d = x.shape
    lam_init = 0.8 - 0.6 * math.exp(-0.3 * l)
    qt, k, vt, xp, gates = _mixer_in(
        x, posf, mod, p['norm_mix_gain'][l][None], invf, p['w_in'][l].astype(BF16),
        p['w_gate'][l].astype(BF16), p['b_gate'][l][None], tm)
    lam_vecs = jnp.stack([p['lambda_q1'][l], p['lambda_k1'][l],
                          p['lambda_q2'][l], p['lambda_k2'][l]])
    o = _diff_attn(lam_vecs, qt, k, vt, p['attn_subln_gain'][l][None], lam_init, tq, 8)
    x1, h2 = _merge_out(
        x, xp, o, gates, mod, p['norm_ffn_gain'][l][None], p['pool_w'][l].astype(BF16),
        p['pool_scale'][l][None], p['w_branch_pool'][l].astype(BF16),
        p['w_branch_attn'][l].astype(BF16), p['w_out'][l].astype(BF16), tm)
    h2f = h2.reshape(bsz * s, d)
    wq = p['peer_w_query'][l].astype(BF16).reshape(d, PEER_HEADS, 2 * PEER_HALF).transpose(1, 0, 2)
    isel, jsel, g = _peer_route(h2f, wq, p['peer_sub_keys'][l].astype(BF16), tb_route)
    return h2f, isel, jsel, g, x1


def kernel(x, c, positions, norm_mix_gain, norm_ffn_gain, w_ada, b_ada, w_in, w_gate, b_gate,
           pool_w, pool_scale, lambda_q1, lambda_k1, lambda_q2, lambda_k2, attn_subln_gain,
           w_branch_pool, w_branch_attn, w_out, peer_w_query, peer_sub_keys, peer_down, peer_up,
           final_norm_gain):
    bsz, s, d = x.shape
    depth = w_in.shape[0]
    assert depth == 1, "the driver below runs one layer and then the final RMSNorm"
    p = dict(norm_mix_gain=norm_mix_gain, norm_ffn_gain=norm_ffn_gain, w_in=w_in, w_gate=w_gate,
             b_gate=b_gate, pool_w=pool_w, pool_scale=pool_scale, lambda_q1=lambda_q1,
             lambda_k1=lambda_k1, lambda_q2=lambda_q2, lambda_k2=lambda_k2,
             attn_subln_gain=attn_subln_gain, w_branch_pool=w_branch_pool,
             w_branch_attn=w_branch_attn, w_out=w_out, peer_w_query=peer_w_query,
             peer_sub_keys=peer_sub_keys)
    tm = min(512, s)
    tq = min(256, s)
    tb_route = 1024
    tb_ffn = 512
    ec = 1024
    inv_freq = ROPE_THETA ** (-jnp.arange(0, ROT_DIM, 2, dtype=F32) / ROT_DIM)
    invf = jnp.tile(inv_freq, LANES // (ROT_DIM // 2))[None]
    posf = positions.astype(F32)[..., None]
    l = 0
    mod = _ada_mod(c, w_ada[l], b_ada[l]).reshape(bsz, N_MOD, d)
    h2f, isel, jsel, g, x1 = _layer(x, posf, invf, mod, l, p, tm, tq, tb_route, tb_ffn, ec)
    down_t = peer_down[l].astype(BF16).T
    y = _peer_ffn(h2f, isel, jsel, g, down_t, peer_up[l].astype(BF16), tb_ffn, ec)
    return _finish(x1, y.reshape(bsz, s, d), mod[:, 5:6, :], final_norm_gain[None], tm)
```

```python
import functools
import math

import jax
import jax.numpy as jnp
from jax import lax
from jax.experimental import pallas as pl
from jax.experimental.pallas import tpu as pltpu

F32 = jnp.float32
BF16 = jnp.bfloat16
I32 = jnp.int32

EPS = 1e-6
NEG_INF = -1e30

N_ATTN_HEADS = 8
ATTN_HEAD_DIM = 64
ATTN_V_DIM = 128
ROT_DIM = 16
ROPE_THETA = 500000.0
POOL_WINDOWS = (2, 4, 8, 16)
POOL_GROUP_DIM = 256
POOL_HALO = 16
N_MOD = 6

PEER_HEADS = 8
N_KEYS = 128
PEER_HALF = 128
PEER_TOPK = 16
N_PAIRS = PEER_HEADS * PEER_TOPK

LANES = 128
MXU_DIM = 256
VMEM_LIMIT = 56 * 1024 * 1024

W_PITCH = 136
ROUTE_PITCH = 136


def _rms(x, eps=EPS):
    return x * lax.rsqrt(jnp.mean(x * x, axis=-1, keepdims=True) + eps)


def _gelu(x):
    return 0.5 * x * (1.0 + lax.erf(x * math.sqrt(0.5)))


def _ada_kernel(c_ref, w_ref, b_ref, o_ref):
    ca = jax.nn.silu(c_ref[...])
    o_ref[...] = jnp.dot(ca, w_ref[...], preferred_element_type=F32,
                         precision=lax.Precision.HIGHEST) + b_ref[...]


def _ada_mod(c, w, b):
    bsz, d = c.shape
    n = w.shape[1]
    tn = 1536
    return pl.pallas_call(
        _ada_kernel,
        grid=(n // tn,),
        in_specs=[pl.BlockSpec((bsz, d), lambda j: (0, 0)),
                  pl.BlockSpec((d, tn), lambda j: (0, j)),
                  pl.BlockSpec((1, tn), lambda j: (0, j))],
        out_specs=pl.BlockSpec((bsz, tn), lambda j: (0, j)),
        out_shape=jax.ShapeDtypeStruct((bsz, n), F32),
        compiler_params=pltpu.CompilerParams(vmem_limit_bytes=VMEM_LIMIT),
        name="ada_mod",
    )(c, w, b.reshape(1, n))


def _mixer_in_kernel(x_ref, pos_ref, mod_ref, gain_ref, invf_ref, win_ref, wg_ref, bg_ref,
                     qt_ref, k_ref, vt_ref, xp_ref, gt_ref):
    d = x_ref.shape[-1]
    x = x_ref[0]
    shift = mod_ref[0, 0:1, :]
    scale = mod_ref[0, 1:2, :]
    h = (_rms(x) * gain_ref[...]) * (1.0 + scale) + shift
    hb = h.astype(BF16)

    ang = pos_ref[0] * invf_ref[...]
    cos = jnp.cos(ang)
    sin = jnp.sin(ang)
    lane = lax.broadcasted_iota(I32, (1, LANES), 1) % ATTN_HEAD_DIM
    half = ROT_DIM // 2
    cm = jnp.where(lane < ROT_DIM, cos, 1.0)
    sa = jnp.where(lane < half, -sin, 0.0)
    sb = jnp.where((lane >= half) & (lane < ROT_DIM), sin, 0.0)

    def rope(t):
        return (t * cm + pltpu.roll(t, LANES - half, 1) * sa + pltpu.roll(t, half, 1) * sb)

    p = jnp.dot(hb, win_ref[:, 0:d], preferred_element_type=F32)
    for hh in range(d // LANES):
        sl = slice(hh * LANES, (hh + 1) * LANES)
        qt_ref[0, sl, :] = (rope(p[:, sl]) * ATTN_HEAD_DIM ** -0.5).T.astype(qt_ref.dtype)
    p = jnp.dot(hb, win_ref[:, d:2 * d], preferred_element_type=F32)
    for hh in range(d // LANES):
        sl = slice(hh * LANES, (hh + 1) * LANES)
        k_ref[0, :, sl] = rope(p[:, sl]).astype(k_ref.dtype)
    p = jnp.dot(hb, win_ref[:, 2 * d:3 * d], preferred_element_type=F32)
    for hh in range(d // LANES):
        sl = slice(hh * LANES, (hh + 1) * LANES)
        vt_ref[0, sl, :] = p[:, sl].T.astype(vt_ref.dtype)
    xp_ref[0] = jnp.dot(hb, win_ref[:, 3 * d:4 * d], preferred_element_type=F32)
    for c in range(2):
        g = jnp.dot(hb, wg_ref[:, c * d:(c + 1) * d], preferred_element_type=F32)
        gt_ref[0, :, c * d:(c + 1) * d] = jax.nn.sigmoid(
            g + bg_ref[:, c * d:(c + 1) * d]).astype(gt_ref.dtype)


def _mixer_in(x, posf, mod, gain, invf, w_in, w_gate, b_gate, tm):
    bsz, s, d = x.shape
    const = lambda b, i: (0, 0)
    tile = lambda b, i: (b, i, 0)
    ttile = lambda b, i: (b, 0, i)
    one = pl.Buffered(1)
    return pl.pallas_call(
        _mixer_in_kernel,
        grid=(bsz, s // tm),
        in_specs=[pl.BlockSpec((1, tm, d), tile),
                  pl.BlockSpec((1, tm, 1), tile),
                  pl.BlockSpec((1, N_MOD, d), lambda b, i: (b, 0, 0)),
                  pl.BlockSpec((1, d), const),
                  pl.BlockSpec((1, LANES), const),
                  pl.BlockSpec(w_in.shape, const, pipeline_mode=one),
                  pl.BlockSpec(w_gate.shape, const, pipeline_mode=one),
                  pl.BlockSpec((1, 2 * d), const)],
        out_specs=[pl.BlockSpec((1, d, tm), ttile), pl.BlockSpec((1, tm, d), tile),
                   pl.BlockSpec((1, d, tm), ttile), pl.BlockSpec((1, tm, d), tile),
                   pl.BlockSpec((1, tm, 2 * d), tile)],
        out_shape=[jax.ShapeDtypeStruct((bsz, d, s), BF16),
                   jax.ShapeDtypeStruct((bsz, s, d), BF16),
                   jax.ShapeDtypeStruct((bsz, d, s), BF16),
                   jax.ShapeDtypeStruct((bsz, s, d), F32),
                   jax.ShapeDtypeStruct((bsz, s, 2 * d), BF16)],
        compiler_params=pltpu.CompilerParams(
            dimension_semantics=("parallel", "parallel"), vmem_limit_bytes=VMEM_LIMIT),
        name="mixer_in",
    )(x, posf, mod, gain, invf, w_in, w_gate, b_gate)


def _attn_kernel(lam_ref, qt_ref, k_ref, vt_ref, gain_ref, o_ref, *, nblk, tq, hps, lam_init):
    sub = lax.broadcasted_iota(I32, (LANES, tq), 0)
    zero = jnp.zeros((LANES, tq), BF16)
    qq = []
    for h in range(hps):
        qt = qt_ref[0, h * LANES:(h + 1) * LANES, :]
        qq.append(jnp.concatenate([jnp.where(sub < ATTN_HEAD_DIM, qt, zero),
                                   jnp.where(sub >= ATTN_HEAD_DIM, qt, zero)], axis=1))
    m = [jnp.full((1, 2 * tq), NEG_INF, F32)] * hps
    l = [jnp.zeros((1, 2 * tq), F32)] * hps
    acc = [jnp.zeros((ATTN_V_DIM, 2 * tq), F32)] * hps

    def scores(j, h):
        return jnp.dot(k_ref[0, j * tq:(j + 1) * tq, h * LANES:(h + 1) * LANES], qq[h],
                       preferred_element_type=F32)

    s_next = [scores(0, h) for h in range(hps)]
    for j in range(nblk):
        for h in range(hps):
            s = s_next[h]
            if j + 1 < nblk:
                s_next[h] = scores(j + 1, h)
            if j == nblk - 1:
                kv = lax.broadcasted_iota(I32, s.shape, 0)
                qr = lax.broadcasted_iota(I32, s.shape, 1) % tq
                s = jnp.where(qr >= kv, s, NEG_INF)
            m_new = jnp.maximum(m[h], jnp.max(s, axis=0, keepdims=True))
            alpha = jnp.exp(m[h] - m_new)
            p = jnp.exp(s - m_new)
            l[h] = alpha * l[h] + jnp.sum(p, axis=0, keepdims=True)
            vt = vt_ref[0, h * LANES:(h + 1) * LANES, j * tq:(j + 1) * tq]
            acc[h] = alpha * acc[h] + jnp.dot(vt, p.astype(BF16),
                                              preferred_element_type=F32)
            m[h] = m_new

    lv = lam_ref[...]
    lam = (jnp.exp(jnp.sum(lv[0:1] * lv[1:2], axis=-1, keepdims=True))
           - jnp.exp(jnp.sum(lv[2:3] * lv[3:4], axis=-1, keepdims=True)) + lam_init)
    for h in range(hps):
        cols = slice(h * LANES, (h + 1) * LANES)
        ot = acc[h] * (1.0 / l[h])
        o = (ot[:, :tq] - lam * ot[:, tq:]).T
        o_ref[0, :, cols] = (_rms(o) * gain_ref[:, cols] * (1.0 - lam_init)).astype(o_ref.dtype)


def _diff_attn(lam_vecs, qt, k, vt, gain, lam_init, tq, hps):
    bsz, s, d = k.shape
    w = hps * LANES
    outs = []
    for qi in range(s // tq):
        kv_len = (qi + 1) * tq
        outs.append(pl.pallas_call(
            functools.partial(_attn_kernel, nblk=qi + 1, tq=tq, hps=hps, lam_init=lam_init),
            grid=(bsz, d // w),
            in_specs=[pl.BlockSpec(lam_vecs.shape, lambda b, h: (0, 0)),
                      pl.BlockSpec((1, w, tq), lambda b, h, qi=qi: (b, h, qi)),
                      pl.BlockSpec((1, kv_len, w), lambda b, h: (b, 0, h)),
                      pl.BlockSpec((1, w, kv_len), lambda b, h: (b, h, 0)),
                      pl.BlockSpec((1, w), lambda b, h: (0, h))],
            out_specs=pl.BlockSpec((1, tq, w), lambda b, h: (b, 0, h)),
            out_shape=jax.ShapeDtypeStruct((bsz, tq, d), BF16),
            compiler_params=pltpu.CompilerParams(
                dimension_semantics=("parallel", "parallel"), vmem_limit_bytes=VMEM_LIMIT),
            name=f"diff_attn_q{qi}",
        )(lam_vecs, qt, k, vt, gain))
    return jnp.concatenate(outs, axis=1)


def _merge_kernel(x_ref, xp_ref, halo_ref, o_ref, gt_ref, mod_ref, gain_ref, pw_ref, ps_ref,
                  wbp_ref, wba_ref, wout_ref, x1_ref, h2_ref, xs_ref, *, tm):
    i = pl.program_id(1)
    d = x_ref.shape[-1]
    halo = halo_ref[0]
    xs_ref[0:POOL_HALO, :] = jnp.where(i > 0, halo, jnp.zeros_like(halo))
    xs_ref[POOL_HALO:, :] = xp_ref[0]
    pos = i * tm + lax.broadcasted_iota(I32, (tm, 1), 0)
    y_pool = None
    ya = []
    for g, w in enumerate(POOL_WINDOWS):
        cols = slice(g * POOL_GROUP_DIM, (g + 1) * POOL_GROUP_DIM)
        tok = xs_ref[POOL_HALO:, cols]
        acc = tok
        for u in range(1, w):
            acc = acc + xs_ref[POOL_HALO - u:POOL_HALO - u + tm, cols]
        cnt = jnp.minimum(pos + 1, w).astype(F32)
        pooled = acc / cnt - tok
        yg = jnp.dot(pooled.astype(BF16), pw_ref[g], preferred_element_type=F32)
        ya.append((yg * ps_ref[:, cols]).astype(BF16))
    ya = jnp.concatenate(ya, axis=-1)
    y_pool = jnp.dot(ya, wbp_ref[...], preferred_element_type=F32)
    y_attn = jnp.dot(o_ref[0], wba_ref[...], preferred_element_type=F32)
    merged = (gt_ref[0, :, 0:d].astype(F32) * y_pool + gt_ref[0, :, d:2 * d].astype(F32) * y_attn)
    mix = jnp.dot(merged.astype(BF16), wout_ref[...], preferred_element_type=F32)
    x1 = x_ref[0] + mod_ref[0, 2:3, :] * mix
    x1_ref[0] = x1
    h2 = (_rms(x1) * gain_ref[...]) * (1.0 + mod_ref[0, 4:5, :]) + mod_ref[0, 3:4, :]
    h2_ref[0] = h2.astype(h2_ref.dtype)


def _merge_out(x, xp, o, gates, mod, gain_f, pool_w, pool_scale, wbp, wba, wout, tm):
    bsz, s, d = x.shape
    const2 = lambda b, i: (0, 0)
    tile = lambda b, i: (b, i, 0)
    hb = tm // POOL_HALO
    one = pl.Buffered(1)
    return pl.pallas_call(
        functools.partial(_merge_kernel, tm=tm),
        grid=(bsz, s // tm),
        in_specs=[pl.BlockSpec((1, tm, d), tile),
                  pl.BlockSpec((1, tm, d), tile),
                  pl.BlockSpec((1, POOL_HALO, d), lambda b, i: (b, jnp.maximum(i * hb - 1, 0), 0)),
                  pl.BlockSpec((1, tm, d), tile),
                  pl.BlockSpec((1, tm, 2 * d), tile),
                  pl.BlockSpec((1, N_MOD, d), lambda b, i: (b, 0, 0)),
                  pl.BlockSpec((1, d), const2),
                  pl.BlockSpec(pool_w.shape, lambda b, i: (0, 0, 0), pipeline_mode=one),
                  pl.BlockSpec((1, d), const2),
                  pl.BlockSpec(wbp.shape, const2, pipeline_mode=one),
                  pl.BlockSpec(wba.shape, const2, pipeline_mode=one),
                  pl.BlockSpec(wout.shape, const2, pipeline_mode=one)],
        out_specs=[pl.BlockSpec((1, tm, d), tile), pl.BlockSpec((1, tm, d), tile)],
        out_shape=[jax.ShapeDtypeStruct((bsz, s, d), F32),
                   jax.ShapeDtypeStruct((bsz, s, d), BF16)],
        scratch_shapes=[pltpu.VMEM((tm + POOL_HALO, d), F32)],
        compiler_params=pltpu.CompilerParams(
            dimension_semantics=("parallel", "parallel"), vmem_limit_bytes=VMEM_LIMIT),
        name="merge_out",
    )(x, xp, xp, o, gates, mod, gain_f, pool_w, pool_scale, wbp, wba, wout)


def _staircase():
    return [(a, b) for a in range(PEER_TOPK) for b in range(PEER_TOPK)
            if (a + 1) * (b + 1) <= PEER_TOPK]


def _tree(op, xs):
    xs = list(xs)
    while len(xs) > 1:
        xs = [op(xs[i], xs[i + 1]) if i + 1 < len(xs) else xs[i] for i in range(0, len(xs), 2)]
    return xs[0]


def _route_kernel(h2_ref, wq_ref, sk_ref, isel_ref, jsel_ref, g_ref,
                  slab_ref, km_ref, sv_ref, si_ref, oi_ref, oj_ref, og_ref, *, tb):
    n_slab = tb // LANES
    big = 1e9
    h2 = h2_ref[...]
    cands = _staircase()

    def head(hh, _):
        q = jnp.dot(h2, wq_ref[hh], preferred_element_type=F32).astype(BF16)
        for half in range(2):
            qh = q[:, half * PEER_HALF:(half + 1) * PEER_HALF]
            st = lax.dot_general(sk_ref[half], qh, (((1,), (1,)), ((), ())),
                                 preferred_element_type=F32)
            for c in range(n_slab):
                slab_ref[c * ROUTE_PITCH:c * ROUTE_PITCH + N_KEYS, :] = (
                    st[:, c * LANES:(c + 1) * LANES])
            for k in range(N_KEYS):
                km_ref[k] = slab_ref[pl.ds(k, n_slab, stride=ROUTE_PITCH), :]

            def extract(r, pos_prev):
                part = [None] * 8
                for k in range(N_KEYS):
                    v = jnp.where(pos_prev == float(k), -jnp.inf, km_ref[k])
                    km_ref[k] = v
                    if part[k % 8] is None:
                        part[k % 8] = (v, jnp.full_like(v, float(k)))
                    else:
                        bm, bk = part[k % 8]
                        part[k % 8] = (jnp.maximum(bm, v), jnp.where(v > bm, float(k), bk))

                def lower_key_wins(x, y):
                    take_y = (y[0] > x[0]) | ((y[0] == x[0]) & (y[1] < x[1]))
                    return jnp.where(take_y, y[0], x[0]), jnp.where(take_y, y[1], x[1])

                m, pos = _tree(lower_key_wins, part)
                sv_ref[half, r] = m
                si_ref[half, r] = pos
                return pos

            lax.fori_loop(0, PEER_TOPK, extract, jnp.full((n_slab, LANES), -1.0, F32))

        sv0 = [sv_ref[0, a] for a in range(PEER_TOPK)]
        sv1 = [sv_ref[1, b] for b in range(PEER_TOPK)]
        cand = [sv0[a] + sv1[b] for a, b in cands]
        code = [float(a * PEER_TOPK + b) for a, b in cands]
        top_s, top_pos = [], []
        for _ in range(PEER_TOPK):
            m = _tree(jnp.maximum, cand)
            pos = _tree(jnp.minimum, [jnp.where(c == m, cd, big) for c, cd in zip(cand, code)])
            top_s.append(m)
            top_pos.append(pos)
            cand = [jnp.where(pos == cd, -jnp.inf, c) for c, cd in zip(cand, code)]
        si0 = [si_ref[0, a] for a in range(PEER_TOPK)]
        si1 = [si_ref[1, b] for b in range(PEER_TOPK)]
        e = [jnp.exp(t - top_s[0]) for t in top_s]
        denom = _tree(jnp.add, e)
        for r in range(PEER_TOPK):
            a_sel = jnp.floor(top_pos[r] * (1.0 / PEER_TOPK))
            b_sel = top_pos[r] - a_sel * PEER_TOPK
            i_sel = si0[0]
            j_sel = si1[0]
            for a in range(1, PEER_TOPK):
                i_sel = jnp.where(a_sel == float(a), si0[a], i_sel)
                j_sel = jnp.where(b_sel == float(a), si1[a], j_sel)
            row = hh * PEER_TOPK + r
            oi_ref[row] = i_sel
            oj_ref[row] = j_sel
            og_ref[row] = e[r] / denom
        return 0

    lax.fori_loop(0, PEER_HEADS, head, 0)

    for src, dst in ((oi_ref, isel_ref), (oj_ref, jsel_ref), (og_ref, g_ref)):
        for p in range(N_PAIRS):
            slab_ref[pl.ds(p, n_slab, stride=ROUTE_PITCH), :] = src[p]
        for c in range(n_slab):
            blk = slab_ref[c * ROUTE_PITCH:c * ROUTE_PITCH + N_PAIRS, :].T
            dst[c * LANES:(c + 1) * LANES, :] = blk.astype(dst.dtype)


def _peer_route(h2, wq, sk, tb):
    t, d = h2.shape
    tile = lambda i: (i, 0)
    one = pl.Buffered(1)
    n_slab = tb // LANES
    vregs = lambda n: pltpu.VMEM((n, n_slab, LANES), F32)
    return pl.pallas_call(
        functools.partial(_route_kernel, tb=tb),
        grid=(t // tb,),
        in_specs=[pl.BlockSpec((tb, d), tile),
                  pl.BlockSpec(wq.shape, lambda i: (0, 0, 0), pipeline_mode=one),
                  pl.BlockSpec(sk.shape, lambda i: (0, 0, 0), pipeline_mode=one)],
        out_specs=[pl.BlockSpec((tb, N_PAIRS), tile)] * 3,
        out_shape=[jax.ShapeDtypeStruct((t, N_PAIRS), I32),
                   jax.ShapeDtypeStruct((t, N_PAIRS), I32),
                   jax.ShapeDtypeStruct((t, N_PAIRS), F32)],
        scratch_shapes=[pltpu.VMEM((n_slab * ROUTE_PITCH, LANES), F32),
                        vregs(N_KEYS),
                        pltpu.VMEM((2, PEER_TOPK, n_slab, LANES), F32),
                        pltpu.VMEM((2, PEER_TOPK, n_slab, LANES), F32),
                        vregs(N_PAIRS), vregs(N_PAIRS), vregs(N_PAIRS)],
        compiler_params=pltpu.CompilerParams(
            dimension_semantics=("parallel",), vmem_limit_bytes=VMEM_LIMIT),
        name="peer_route",
    )(h2, wq, sk)


def _ffn_kernel(h2_ref, isel_ref, jsel_ref, g_ref, dnt_ref, up_ref, out_ref, w_ref, *, tb, ec):
    e = pl.program_id(1)

    @pl.when(e == 0)
    def _build():
        out_ref[...] = jnp.zeros_like(out_ref)
        key = lax.broadcasted_iota(I32, (N_KEYS, N_PAIRS), 0)

        def tok(t, _):
            ii = jnp.broadcast_to(isel_ref[pl.ds(t, 1), :], (N_KEYS, N_PAIRS))
            jj = jnp.broadcast_to(jsel_ref[pl.ds(t, 1), :], (N_KEYS, N_PAIRS))
            gg = jnp.broadcast_to(g_ref[pl.ds(t, 1), :], (N_KEYS, N_PAIRS))
            a = jnp.where(ii == key, 1.0, 0.0).astype(BF16)
            b = jnp.where(jj == key, gg, 0.0).astype(BF16)
            w_ref[pl.ds(pl.multiple_of(t * W_PITCH, 8), N_KEYS), :] = lax.dot_general(
                a, b, (((1,), (1,)), ((), ())), preferred_element_type=F32)
            return 0

        lax.fori_loop(0, tb, tok, 0, unroll=32)

    h2 = h2_ref[...]
    tile0 = e * (ec // LANES)
    z = []
    for c in range(ec // MXU_DIM):
        cols = slice(c * MXU_DIM, (c + 1) * MXU_DIM)
        pre = jnp.dot(h2, dnt_ref[:, cols], preferred_element_type=F32)
        w = jnp.concatenate(
            [w_ref[pl.ds(tile0 + c * (MXU_DIM // LANES) + u, tb, stride=W_PITCH), :]
             for u in range(MXU_DIM // LANES)], axis=-1)
        z.append((w * _gelu(pre)).astype(BF16))
    out_ref[...] += jnp.dot(jnp.concatenate(z, axis=-1), up_ref[...],
                            preferred_element_type=F32)


def _peer_ffn(h2, isel, jsel, g, down_t, up, tb, ec):
    t, d = h2.shape
    n_exp = up.shape[0]
    tile = lambda i, e: (i, 0)
    return pl.pallas_call(
        functools.partial(_ffn_kernel, tb=tb, ec=ec),
        grid=(t // tb, n_exp // ec),
        in_specs=[pl.BlockSpec((tb, d), tile),
                  pl.BlockSpec((tb, N_PAIRS), tile),
                  pl.BlockSpec((tb, N_PAIRS), tile),
                  pl.BlockSpec((tb, N_PAIRS), tile),
                  pl.BlockSpec((d, ec), lambda i, e: (0, e)),
                  pl.BlockSpec((ec, d), lambda i, e: (e, 0))],
        out_specs=pl.BlockSpec((tb, d), tile),
        out_shape=jax.ShapeDtypeStruct((t, d), F32),
        scratch_shapes=[pltpu.VMEM((tb * W_PITCH, LANES), F32)],
        compiler_params=pltpu.CompilerParams(
            dimension_semantics=("parallel", "arbitrary"), vmem_limit_bytes=VMEM_LIMIT),
        name="peer_ffn",
    )(h2, isel, jsel, g, down_t, up)


def _finish_kernel(x1_ref, y_ref, gate_ref, fg_ref, o_ref):
    x2 = x1_ref[0] + gate_ref[0] * y_ref[0]
    o_ref[0] = _rms(x2) * fg_ref[...]


def _finish(x1, y, gate_f, final_gain, tm):
    bsz, s, d = x1.shape
    tile = lambda b, i: (b, i, 0)
    return pl.pallas_call(
        _finish_kernel,
        grid=(bsz, s // tm),
        in_specs=[pl.BlockSpec((1, tm, d), tile),
                  pl.BlockSpec((1, tm, d), tile),
                  pl.BlockSpec((1, 1, d), lambda b, i: (b, 0, 0)),
                  pl.BlockSpec((1, d), lambda b, i: (0, 0))],
        out_specs=pl.BlockSpec((1, tm, d), tile),
        out_shape=jax.ShapeDtypeStruct((bsz, s, d), F32),
        compiler_params=pltpu.CompilerParams(
            dimension_semantics=("parallel", "parallel"), vmem_limit_bytes=VMEM_LIMIT),
        name="finish",
    )(x1, y, gate_f, final_gain)


def _layer(x, posf, invf, mod, l, p, tm, tq, tb_route, tb_ffn, ec):
    bsz, s, d = x.shape
    lam_init = 0.8 - 0.6 * math.exp(-0.3 * l)
    qt, k, vt, xp, gates = _mixer_in(
        x, posf, mod, p['norm_mix_gain'][l][None], invf, p['w_in'][l].astype(BF16),
        p['w_gate'][l].astype(BF16), p['b_gate'][l][None], tm)
    lam_vecs = jnp.stack([p['lambda_q1'][l], p['lambda_k1'][l],
                          p['lambda_q2'][l], p['lambda_k2'][l]])
    o = _diff_attn(lam_vecs, qt, k, vt, p['attn_subln_gain'][l][None], lam_init, tq, 4)
    x1, h2 = _merge_out(
        x, xp, o, gates, mod, p['norm_ffn_gain'][l][None], p['pool_w'][l].astype(BF16),
        p['pool_scale'][l][None], p['w_branch_pool'][l].astype(BF16),
        p['w_branch_attn'][l].astype(BF16), p['w_out'][l].astype(BF16), tm)
    h2f = h2.reshape(bsz * s, d)
    wq = p['peer_w_query'][l].astype(BF16).reshape(d, PEER_HEADS, 2 * PEER_HALF).transpose(1, 0, 2)
    isel, jsel, g = _peer_route(h2f, wq, p['peer_sub_keys'][l].astype(BF16), tb_route)
    return h2f, isel, jsel, g, x1


def kernel(x, c, positions, norm_mix_gain, norm_ffn_gain, w_ada, b_ada, w_in, w_gate, b_gate,
           pool_w, pool_scale, lambda_q1, lambda_k1, lambda_q2, lambda_k2, attn_subln_gain,
           w_branch_pool, w_branch_attn, w_out, peer_w_query, peer_sub_keys, peer_down, peer_up,
           final_norm_gain):
    bsz, s, d = x.shape
    depth = w_in.shape[0]
    assert depth == 1, "the driver below runs one layer and then the final RMSNorm"
    p = dict(norm_mix_gain=norm_mix_gain, norm_ffn_gain=norm_ffn_gain, w_in=w_in, w_gate=w_gate,
             b_gate=b_gate, pool_w=pool_w, pool_scale=pool_scale, lambda_q1=lambda_q1,
             lambda_k1=lambda_k1, lambda_q2=lambda_q2, lambda_k2=lambda_k2,
             attn_subln_gain=attn_subln_gain, w_branch_pool=w_branch_pool,
             w_branch_attn=w_branch_attn, w_out=w_out, peer_w_query=peer_w_query,
             peer_sub_keys=peer_sub_keys)
    tm = min(512, s)
    tq = min(512, s)
    tb_route = 1024
    tb_ffn = 512
    ec = 1024
    inv_freq = ROPE_THETA ** (-jnp.arange(0, ROT_DIM, 2, dtype=F32) / ROT_DIM)
    invf = jnp.tile(inv_freq, LANES // (ROT_DIM // 2))[None]
    posf = positions.astype(F32)[..., None]
    l = 0
    mod = _ada_mod(c, w_ada[l], b_ada[l]).reshape(bsz, N_MOD, d)
    h2f, isel, jsel, g, x1 = _layer(x, posf, invf, mod, l, p, tm, tq, tb_route, tb_ffn, ec)
    down_t = peer_down[l].astype(BF16).T
    y = _peer_ffn(h2f, isel, jsel, g, down_t, peer_up[l].astype(BF16), tb_ffn, ec)
    return _finish(x1, y.reshape(bsz, s, d), mod[:, 5:6, :], final_norm_gain[None], tm)
```
